```python
import jax
import jax.numpy as jnp
from jax import lax
import numpy as np

D_MODEL = 1024
BATCH = 8
SEQ = 2048
DEPTH = 2
DEC_BATCH = 4
DEC_SEQ = 4096
PAST_LEN = 128

GRID_W = 64
HEAD_DIM = 64
BRANCH_WIDTH = 256
N_BRANCHES = 4
NA_HEADS = 4
NA_KH_MAX = 8
NA_KW = 16
NA_QBLK_W = 16
NA_KBLK_W = NA_QBLK_W + NA_KW
ML_HEADS = 4
ML_CHUNK = 64
ML_N_GATES = 4
CONV_WIDTH = 31
ATT_Q_HEADS = 4
ATT_KV_HEADS = 2
ATT_GROUP = ATT_Q_HEADS // ATT_KV_HEADS
ATT_QBLK = 128
ROPE_THETA = 10000.0
ROPE_AXIS_DIM = HEAD_DIM // 2
MOE_GROUPS = 4
MOE_EXPERTS_PER_GROUP = 4
MOE_N_EXPERTS = MOE_GROUPS * MOE_EXPERTS_PER_GROUP
MOE_TOP_K = 2
MOE_D_EXPERT = 256
NORM_EPS = 1e-6
NEG_INF = -1e30
F32 = jnp.float32

NA_COLS = 3 * NA_HEADS * HEAD_DIM
ML_COLS = 4 * ML_HEADS * HEAD_DIM + ML_N_GATES * ML_HEADS
CONV_COLS = 2 * BRANCH_WIDTH
ATT_COLS = (ATT_Q_HEADS + 2 * ATT_KV_HEADS) * HEAD_DIM
IN_COLS = NA_COLS + ML_COLS + CONV_COLS + ATT_COLS

kernel_name = 'hybrid_bidir_encoder_na_mlstm_conv_gqa_hmoe'


def rms_norm(x, g):
    xf = x.astype(F32)
    y = xf * lax.rsqrt(jnp.mean(xf * xf, axis=-1, keepdims=True) + NORM_EPS)
    return (y * g.astype(F32)).astype(x.dtype)


def layer_norm(x, g, b):
    xf = x.astype(F32)
    xc = xf - jnp.mean(xf, axis=-1, keepdims=True)
    y = xc * lax.rsqrt(jnp.mean(xc * xc, axis=-1, keepdims=True) + NORM_EPS)
    return (y * g.astype(F32) + b.astype(F32)).astype(x.dtype)


def neighbourhood_attention(q, k, v, rpb):
    bsz, seq = q.shape[0], q.shape[1]
    rows = seq // GRID_W
    kh = min(NA_KH_MAX, rows)
    n_cb = GRID_W // NA_QBLK_W
    r = jnp.arange(rows)
    row_idx = jnp.clip(r - kh // 2, 0, rows - kh)[:, None] + jnp.arange(kh)[None, :]
    cb = jnp.arange(n_cb)
    blk_start = jnp.clip(cb * NA_QBLK_W - NA_KW // 2, 0, GRID_W - NA_KBLK_W)
    key_col = blk_start[:, None] + jnp.arange(NA_KBLK_W)[None, :]
    q_col = cb[:, None] * NA_QBLK_W + jnp.arange(NA_QBLK_W)[None, :]
    win_start = jnp.clip(q_col - NA_KW // 2, 0, GRID_W - NA_KW)
    kc = key_col[:, None, :]
    in_win = (kc >= win_start[..., None]) & (kc < win_start[..., None] + NA_KW)
    col_off = jnp.clip(kc - q_col[..., None] + NA_KW - 1, 0, 2 * NA_KW - 2)
    row_off = row_idx - r[:, None] + NA_KH_MAX - 1
    bias = rpb.astype(F32)[:, row_off][..., col_off]
    bias = jnp.where(in_win[None, None, None], bias, NEG_INF).transpose(0, 1, 3, 4, 2, 5)
    qg = q.reshape(bsz, rows, n_cb, NA_QBLK_W, NA_HEADS, HEAD_DIM)
    kg = k.reshape(bsz, rows, GRID_W, NA_HEADS, HEAD_DIM)
    vg = v.reshape(bsz, rows, GRID_W, NA_HEADS, HEAD_DIM)
    ri = row_idx[:, :, None, None]
    ci = key_col[None, None]
    k_blk = kg[:, ri, ci]
    v_blk = vg[:, ri, ci]
    s = jnp.einsum('brnqhd,brjnkhd->bhrnqjk', qg, k_blk).astype(F32) * HEAD_DIM ** -0.5 + bias[None]
    p = jax.nn.softmax(s.reshape(*s.shape[:-2], -1), axis=-1).reshape(s.shape)
    o = jnp.einsum('bhrnqjk,brjnkhd->brnqhd', p.astype(v.dtype), v_blk)
    return o.reshape(bsz, seq, NA_HEADS * HEAD_DIM)


def mlstm_scan(q, k, v, i_pre, log_f):
    bsz, heads, seq, dh = q.shape
    n_chunks = seq // ML_CHUNK

    def chunks(a):
        return jnp.moveaxis(a.reshape(bsz, heads, n_chunks, ML_CHUNK, *a.shape[3:]), 2, 0)

    causal = jnp.tril(jnp.ones((ML_CHUNK, ML_CHUNK), dtype=bool))

    def step(carry, xs):
        c_mem, n_mem, m_run = carry
        qc, kc, vc, ic, fc = xs
        b = jnp.cumsum(fc, axis=-1)
        log_d = jnp.where(causal, b[..., :, None] - b[..., None, :] + ic[..., None, :], NEG_INF)
        inter = b + m_run[..., None]
        m_row = jnp.maximum(inter, jnp.max(log_d, axis=-1))
        d = jnp.exp(log_d - m_row[..., None])
        w_inter = jnp.exp(inter - m_row)
        s = jnp.einsum('bhld,bhsd->bhls', qc, kc) * d
        num = jnp.einsum('bhls,bhsd->bhld', s, vc) + w_inter[..., None] * jnp.einsum('bhvk,bhlk->bhlv', c_mem, qc)
        den = jnp.sum(s, axis=-1) + w_inter * jnp.einsum('bhk,bhlk->bhl', n_mem, qc)
        h = num / jnp.maximum(jnp.abs(den), jnp.exp(-m_row))[..., None]
        b_end = b[..., -1]
        log_w = b_end[..., None] - b + ic
        m_new = jnp.maximum(b_end + m_run, jnp.max(log_w, axis=-1))
        w = jnp.exp(log_w - m_new[..., None])
        decay = jnp.exp(b_end + m_run - m_new)
        c_new = decay[..., None, None] * c_mem + jnp.einsum('bhl,bhlv,bhlk->bhvk', w, vc, kc)
        n_new = decay[..., None] * n_mem + jnp.einsum('bhl,bhlk->bhk', w, kc)
        return (c_new, n_new, m_new), h

    init = (jnp.zeros((bsz, heads, dh, dh), F32), jnp.zeros((bsz, heads, dh), F32),
            jnp.full((bsz, heads), NEG_INF, F32))
    _, hs = lax.scan(step, init, (chunks(q), chunks(k), chunks(v), chunks(i_pre), chunks(log_f)))
    return jnp.moveaxis(hs, 0, 2).reshape(bsz, heads, seq, dh)


def conformer_conv(u, w_dw, b_dw, g, beta):
    a, gate = jnp.split(u, 2, axis=-1)
    x = a * jax.nn.sigmoid(gate)
    x = lax.conv_general_dilated(x, w_dw[:, None, :].astype(x.dtype), window_strides=(1,),
                                 padding=[(CONV_WIDTH // 2, CONV_WIDTH // 2)],
                                 dimension_numbers=('NWC', 'WIO', 'NWC'),
                                 feature_group_count=BRANCH_WIDTH) + b_dw
    return jax.nn.silu(layer_norm(x, g, beta))


def axial_rope(x, seq):
    t = jnp.arange(seq)
    pos = jnp.stack([t // GRID_W, t % GRID_W], axis=-1).astype(F32)
    inv = ROPE_THETA ** (-jnp.arange(0, ROPE_AXIS_DIM, 2, dtype=F32) / ROPE_AXIS_DIM)
    ang = pos[..., None] * inv
    ang = jnp.concatenate([ang, ang], axis=-1)[:, None]
    xs = x.astype(F32).reshape(*x.shape[:-1], 2, ROPE_AXIS_DIM)
    half = ROPE_AXIS_DIM // 2
    rot = jnp.concatenate([-xs[..., half:], xs[..., :half]], axis=-1)
    return (xs * jnp.cos(ang) + rot * jnp.sin(ang)).reshape(x.shape).astype(x.dtype)


def blocked_gqa(q, k, v):
    bsz, seq = q.shape[0], q.shape[1]
    qb = q.reshape(bsz, seq // ATT_QBLK, ATT_QBLK, ATT_KV_HEADS, ATT_GROUP, HEAD_DIM).transpose(1, 0, 2, 3, 4, 5)

    def one_block(q_blk):
        s = jnp.einsum('bqhgd,bkhd->bhgqk', q_blk, k).astype(F32) * HEAD_DIM ** -0.5
        p = jax.nn.softmax(s, axis=-1).astype(v.dtype)
        return jnp.einsum('bhgqk,bkhd->bqhgd', p, v)

    o = lax.map(one_block, qb)
    return o.transpose(1, 0, 2, 3, 4, 5).reshape(bsz, seq, ATT_Q_HEADS * HEAD_DIM)


def hybrid_mixer(h, w_in, na_rpb, ml_gate_b, ml_norm_g, conv_dw_w, conv_dw_b, conv_norm_g, conv_norm_b,
                 att_q_norm_g, att_k_norm_g, w_branch, w_gate, b_gate, w_out):
    bsz, seq, _ = h.shape
    u = jnp.einsum('bsd,dc->bsc', h, w_in)
    u_na, u_ml, u_cv, u_at = jnp.split(u, [NA_COLS, NA_COLS + ML_COLS, NA_COLS + ML_COLS + CONV_COLS], axis=-1)

    qkv_a = u_na.reshape(bsz, seq, 3, NA_HEADS, HEAD_DIM)
    o_na = neighbourhood_attention(qkv_a[:, :, 0], qkv_a[:, :, 1], qkv_a[:, :, 2], na_rpb)

    qkvo = u_ml[..., :4 * BRANCH_WIDTH].reshape(bsz, seq, 4, ML_HEADS, HEAD_DIM)
    gates = u_ml[..., 4 * BRANCH_WIDTH:].reshape(bsz, seq, ML_N_GATES, ML_HEADS).astype(F32) + ml_gate_b.astype(F32)
    gates = gates.transpose(2, 0, 3, 1)
    qm = qkvo[:, :, 0].astype(F32).transpose(0, 2, 1, 3)
    km = qkvo[:, :, 1].astype(F32).transpose(0, 2, 1, 3) * HEAD_DIM ** -0.5
    vm = qkvo[:, :, 2].astype(F32).transpose(0, 2, 1, 3)
    h_fwd = mlstm_scan(qm, km, vm, gates[0], jax.nn.log_sigmoid(gates[1]))
    h_bwd = jnp.flip(mlstm_scan(jnp.flip(qm, 2), jnp.flip(km, 2), jnp.flip(vm, 2),
                                jnp.flip(gates[2], 2), jnp.flip(jax.nn.log_sigmoid(gates[3]), 2)), 2)
    hm = rms_norm((h_fwd + h_bwd).transpose(0, 2, 1, 3), ml_norm_g.reshape(ML_HEADS, HEAD_DIM))
    o_ml = (hm * jax.nn.sigmoid(qkvo[:, :, 3].astype(F32))).reshape(bsz, seq, BRANCH_WIDTH).astype(h.dtype)

    o_cv = conformer_conv(u_cv, conv_dw_w, conv_dw_b, conv_norm_g, conv_norm_b)

    nq = ATT_Q_HEADS * HEAD_DIM
    nk = ATT_KV_HEADS * HEAD_DIM
    qd = u_at[..., :nq].reshape(bsz, seq, ATT_Q_HEADS, HEAD_DIM)
    kd = u_at[..., nq:nq + nk].reshape(bsz, seq, ATT_KV_HEADS, HEAD_DIM)
    vd = u_at[..., nq + nk:].reshape(bsz, seq, ATT_KV_HEADS, HEAD_DIM)
    qd = axial_rope(rms_norm(qd, att_q_norm_g), seq)
    kd = axial_rope(rms_norm(kd, att_k_norm_g), seq)
    o_at = blocked_gqa(qd, kd, vd)

    branches = jnp.stack([o_na, o_ml.astype(o_na.dtype), o_cv.astype(o_na.dtype), o_at.astype(o_na.dtype)], axis=2)
    proj = jnp.einsum('bsnc,ncd->bsnd', branches, w_branch).astype(F32)
    g = jax.nn.sigmoid((jnp.einsum('bsd,dc->bsc', h, w_gate) + b_gate).astype(F32)).reshape(bsz, seq, N_BRANCHES, D_MODEL)
    merged = jnp.sum(g * proj, axis=2).astype(h.dtype)
    return jnp.einsum('bsd,de->bse', merged, w_out)


def hierarchical_moe(h, w_group, b_group, w_expert, b_expert, w1, w3, w2):
    g_logits = (jnp.einsum('bsd,dg->bsg', h, w_group) + b_group).astype(F32)
    g_prob = jax.nn.softmax(g_logits, axis=-1)
    g_idx = jnp.argmax(g_logits, axis=-1)
    g_gate = jnp.take_along_axis(g_prob, g_idx[..., None], axis=-1)[..., 0]
    e_logits = (jnp.einsum('bsd,dge->bsge', h, w_expert) + b_expert).astype(F32)
    e_in = jnp.take_along_axis(e_logits, g_idx[..., None, None], axis=2)[..., 0, :]
    top_val, top_idx = lax.top_k(e_in, MOE_TOP_K)
    top_w = jax.nn.softmax(top_val, axis=-1) * g_gate[..., None]
    expert_id = g_idx[..., None] * MOE_EXPERTS_PER_GROUP + top_idx
    combine = jnp.sum(jax.nn.one_hot(expert_id, MOE_N_EXPERTS, dtype=F32) * top_w[..., None], axis=-2)
    a = jnp.einsum('bsd,edf->bsef', h, w1)
    b = jnp.einsum('bsd,edf->bsef', h, w3)
    hid = jax.nn.silu(a) * b * combine[..., None].astype(h.dtype)
    return jnp.einsum('bsef,efd->bsd', hid, w2)


def encoder_trunk(x, norm_mix_g, w_in, na_rpb, ml_gate_b, ml_norm_g, conv_dw_w, conv_dw_b, conv_norm_g,
                  conv_norm_b, att_q_norm_g, att_k_norm_g, w_branch, w_gate, b_gate, w_out, norm_ffn_g,
                  moe_w_group, moe_b_group, moe_w_expert, moe_b_expert, moe_w1, moe_w3, moe_w2, final_norm_g):
    for l in range(DEPTH):
        h = rms_norm(x, norm_mix_g[l])
        x = x + hybrid_mixer(h, w_in[l], na_rpb[l], ml_gate_b[l], ml_norm_g[l], conv_dw_w[l], conv_dw_b[l],
                             conv_norm_g[l], conv_norm_b[l], att_q_norm_g[l], att_k_norm_g[l], w_branch[l],
                             w_gate[l], b_gate[l], w_out[l]).astype(x.dtype)
        h = rms_norm(x, norm_ffn_g[l])
        x = x + hierarchical_moe(h, moe_w_group[l], moe_b_group[l], moe_w_expert[l], moe_b_expert[l],
                                 moe_w1[l], moe_w3[l], moe_w2[l]).astype(x.dtype)
    return rms_norm(x, final_norm_g)


def setup_inputs(seed: int = 0) -> dict:
    key = jax.random.key(seed)
    ks = jax.random.split(key, 26)

    def nrm(k, shape, scale):
        return jax.random.normal(k, shape, F32) * scale

    L, D = DEPTH, D_MODEL
    f_base = jnp.array([0.0, 1.0, 0.0, 1.0], F32)[:, None] * jnp.linspace(3.0, 6.0, ML_HEADS, dtype=F32)[None, :]
    return {
        'x_prompt': nrm(ks[0], (BATCH, SEQ, D), 1.0),
        'x_sample': nrm(ks[1], (DEC_BATCH, DEC_SEQ, D), 1.0),
        'norm_mix_g': 1.0 + nrm(ks[2], (L, D), 0.05),
        'w_in': nrm(ks[3], (L, D, IN_COLS), D ** -0.5),
        'na_rpb': nrm(ks[4], (L, NA_HEADS, 2 * NA_KH_MAX - 1, 2 * NA_KW - 1), 0.2),
        'ml_gate_b': f_base[None] + nrm(ks[5], (L, ML_N_GATES, ML_HEADS), 0.1),
        'ml_norm_g': 1.0 + nrm(ks[6], (L, BRANCH_WIDTH), 0.05),
        'conv_dw_w': nrm(ks[7], (L, CONV_WIDTH, BRANCH_WIDTH), CONV_WIDTH ** -0.5),
        'conv_dw_b': nrm(ks[8], (L, BRANCH_WIDTH), 0.02),
        'conv_norm_g': 1.0 + nrm(ks[9], (L, BRANCH_WIDTH), 0.05),
        'conv_norm_b': nrm(ks[10], (L, BRANCH_WIDTH), 0.02),
        'att_q_norm_g': 1.0 + nrm(ks[11], (L, HEAD_DIM), 0.05),
        'att_k_norm_g': 1.0 + nrm(ks[12], (L, HEAD_DIM), 0.05),
        'w_branch': nrm(ks[13], (L, N_BRANCHES, BRANCH_WIDTH, D), BRANCH_WIDTH ** -0.5),
        'w_gate': nrm(ks[14], (L, D, N_BRANCHES * D), D ** -0.5),
        'b_gate': nrm(ks[15], (L, N_BRANCHES * D), 0.02),
        'w_out': nrm(ks[16], (L, D, D), D ** -0.5),
        'norm_ffn_g': 1.0 + nrm(ks[17], (L, D), 0.05),
        'moe_w_group': nrm(ks[18], (L, D, MOE_GROUPS), D ** -0.5),
        'moe_b_group': nrm(ks[19], (L, MOE_GROUPS), 0.01),
        'moe_w_expert': nrm(ks[20], (L, D, MOE_GROUPS, MOE_EXPERTS_PER_GROUP), D ** -0.5),
        'moe_b_expert': nrm(ks[21], (L, MOE_GROUPS, MOE_EXPERTS_PER_GROUP), 0.01),
        'moe_w1': nrm(ks[22], (L, MOE_N_EXPERTS, D, MOE_D_EXPERT), D ** -0.5),
        'moe_w3': nrm(ks[23], (L, MOE_N_EXPERTS, D, MOE_D_EXPERT), D ** -0.5),
        'moe_w2': nrm(ks[24], (L, MOE_N_EXPERTS, MOE_D_EXPERT, D), MOE_D_EXPERT ** -0.5),
        'final_norm_g': 1.0 + nrm(ks[25], (D,), 0.05),
    }


def reference(x_prompt, x_sample, norm_mix_g, w_in, na_rpb, ml_gate_b, ml_norm_g, conv_dw_w, conv_dw_b,
              conv_norm_g, conv_norm_b, att_q_norm_g, att_k_norm_g, w_branch, w_gate, b_gate, w_out, norm_ffn_g,
              moe_w_group, moe_b_group, moe_w_expert, moe_b_expert, moe_w1, moe_w3, moe_w2, final_norm_g):
    y_prompt = encoder_trunk(x_prompt, norm_mix_g, w_in, na_rpb, ml_gate_b, ml_norm_g, conv_dw_w, conv_dw_b,
                             conv_norm_g, conv_norm_b, att_q_norm_g, att_k_norm_g, w_branch, w_gate, b_gate,
                             w_out, norm_ffn_g, moe_w_group, moe_b_group, moe_w_expert, moe_b_expert, moe_w1,
                             moe_w3, moe_w2, final_norm_g)
    y_sample = encoder_trunk(x_sample, norm_mix_g, w_in, na_rpb, ml_gate_b, ml_norm_g, conv_dw_w, conv_dw_b,
                             conv_norm_g, conv_norm_b, att_q_norm_g, att_k_norm_g, w_branch, w_gate, b_gate,
                             w_out, norm_ffn_g, moe_w_group, moe_b_group, moe_w_expert, moe_b_expert, moe_w1,
                             moe_w3, moe_w2, final_norm_g)
    return (y_prompt, y_sample)
```

```python
import functools

import jax
import jax.numpy as jnp
import numpy as np
from jax import lax
from jax.experimental import pallas as pl
from jax.experimental.pallas import tpu as pltpu

F32 = jnp.float32
BF16 = jnp.bfloat16

D_MODEL = 1024
GRID_W = 64
HEAD_DIM = 64
BRANCH_WIDTH = 256
N_BRANCHES = 4
N_HEADS = 4
NA_KH = 8
NA_KW = 16
ML_CHUNK = 64
ML_N_GATES = 4
CONV_WIDTH = 31
ATT_KV_HEADS = 2
ROPE_THETA = 10000.0
ROPE_AXIS_DIM = HEAD_DIM // 2
MOE_GROUPS = 4
MOE_EXPERTS_PER_GROUP = 4
MOE_N_EXPERTS = 16
MOE_D_EXPERT = 256
NORM_EPS = 1e-6
NEG_INF = -1e30

NA_COLS = 3 * BRANCH_WIDTH
ML_COLS = 4 * BRANCH_WIDTH + ML_N_GATES * N_HEADS
CONV_COLS = 2 * BRANCH_WIDTH
ATT_COLS = (N_HEADS + 2 * ATT_KV_HEADS) * HEAD_DIM
QK_COLS = (N_HEADS + ATT_KV_HEADS) * HEAD_DIM

LANES = 128
ROUTER_LANES = LANES
ROUTER_E0 = MOE_GROUPS
V7X_VMEM_BYTES = 64 * 1024 * 1024

TOKEN_TILE = 512


def _cparams(semantics, vmem_mb):
    assert vmem_mb * 1024 * 1024 < V7X_VMEM_BYTES
    return pltpu.CompilerParams(dimension_semantics=semantics, vmem_limit_bytes=vmem_mb * 1024 * 1024)


def _const_spec(shape):
    nd = len(shape)
    return pl.BlockSpec(shape, lambda *_: (0,) * nd)


def _rms(x, g):
    ms = jnp.mean(x * x, axis=-1, keepdims=True)
    return x * lax.rsqrt(ms + NORM_EPS) * g


def _inproj_body(x_ref, g_ref, wna_ref, wml_ref, wg_ref, wcv_ref, wat_ref, ones_ref, qkg_ref, cos_ref, sin_ref,
                 una_ref, uml_ref, gates_ref, ucv_ref, qk_ref, v_ref):
    h = _rms(x_ref[...], g_ref[...]).astype(BF16)
    una_ref[...] = jnp.dot(h, wna_ref[...], preferred_element_type=F32).astype(BF16)
    uml_ref[...] = jnp.dot(h, wml_ref[...], preferred_element_type=F32).astype(BF16)
    gates_ref[...] = jnp.dot(h, wg_ref[...], preferred_element_type=F32)
    ucv_ref[...] = jnp.dot(h, wcv_ref[...], preferred_element_type=F32).astype(BF16)
    ua = jnp.dot(h, wat_ref[...], preferred_element_type=F32)
    qk = ua[:, :QK_COLS]
    ms = jnp.dot((qk * qk).astype(BF16), ones_ref[...], preferred_element_type=F32) * (1.0 / HEAD_DIM)
    qn = qk * lax.rsqrt(ms + NORM_EPS) * qkg_ref[...]
    half = ROPE_AXIS_DIM // 2
    rot = []
    for c in range(QK_COLS // LANES):
        xc = qn[:, c * LANES:(c + 1) * LANES]
        lane = lax.broadcasted_iota(jnp.int32, xc.shape, 1)
        rot.append(jnp.where((lane % ROPE_AXIS_DIM) < half, pltpu.roll(xc, LANES - half, 1), pltpu.roll(xc, half, 1)))
    rot = jnp.concatenate(rot, axis=1)
    qk_ref[...] = (qn * cos_ref[...] + rot * sin_ref[...]).astype(BF16)
    v_ref[...] = ua[:, QK_COLS:].astype(BF16)


def _inproj(x, g, wna, wml, wg, wcv, wat, ones_qk, qkg, cos_t, sin_t, seq):
    n_tok = x.shape[0]
    tm = TOKEN_TILE
    tiles_per_seq = seq // tm
    row = lambda i: (i, 0)
    pos = lambda i: (i % tiles_per_seq, 0)
    outs = [
        jax.ShapeDtypeStruct((n_tok, NA_COLS), BF16),
        jax.ShapeDtypeStruct((n_tok, 4 * BRANCH_WIDTH), BF16),
        jax.ShapeDtypeStruct((n_tok, LANES), F32),
        jax.ShapeDtypeStruct((n_tok, CONV_COLS), BF16),
        jax.ShapeDtypeStruct((n_tok, QK_COLS), BF16),
        jax.ShapeDtypeStruct((n_tok, ATT_KV_HEADS * HEAD_DIM), BF16),
    ]
    return pl.pallas_call(
        _inproj_body,
        grid=(n_tok // tm,),
        in_specs=[
            pl.BlockSpec((tm, D_MODEL), row),
            _const_spec((1, D_MODEL)),
            _const_spec(wna.shape), _const_spec(wml.shape), _const_spec(wg.shape), _const_spec(wcv.shape),
            _const_spec(wat.shape), _const_spec(ones_qk.shape), _const_spec(qkg.shape),
            pl.BlockSpec((tm, QK_COLS), pos), pl.BlockSpec((tm, QK_COLS), pos),
        ],
        out_specs=[pl.BlockSpec((tm, o.shape[1]), row) for o in outs],
        out_shape=outs,
        compiler_params=_cparams(("parallel",), 48),
        name="inproj",
    )(x, g, wna, wml, wg, wcv, wat, ones_qk, qkg, cos_t, sin_t)


def _block_diag_mask(rows, cols):
    r = lax.broadcasted_iota(jnp.int32, (rows, cols), 0) // HEAD_DIM
    c = lax.broadcasted_iota(jnp.int32, (rows, cols), 1) // HEAD_DIM
    return r == (c % N_HEADS)


def _tile_heads(x):
    t = jnp.concatenate([x] * N_HEADS, axis=0)
    return jnp.where(_block_diag_mask(N_HEADS * HEAD_DIM, BRANCH_WIDTH), t, jnp.zeros_like(t))


def _na_body(q_ref, k_ref, v_ref, bias_ref, o_ref, *, rows):
    n_keys = NA_KH * GRID_W

    def one_row(r, carry):
        r0 = jnp.clip(r - NA_KH // 2, 0, rows - NA_KH)
        delta = r0 - r + NA_KH - 1
        q = q_ref[pl.ds(pl.multiple_of(r * GRID_W, GRID_W), GRID_W), :]
        kk = k_ref[pl.ds(pl.multiple_of(r0 * GRID_W, GRID_W), n_keys), :]
        vv = v_ref[pl.ds(pl.multiple_of(r0 * GRID_W, GRID_W), n_keys), :]
        s = lax.dot_general(_tile_heads(q), kk, (((1,), (1,)), ((), ())), preferred_element_type=F32)
        s = s + bias_ref[delta]
        m = jnp.max(s, axis=1, keepdims=True)
        p = jnp.exp(s - m)
        l = jnp.sum(p, axis=1, keepdims=True)
        of = jnp.dot(p.astype(BF16), vv, preferred_element_type=F32) / l
        of = jnp.where(_block_diag_mask(N_HEADS * HEAD_DIM, BRANCH_WIDTH), of, 0.0)
        o = of[0:HEAD_DIM] + of[HEAD_DIM:2 * HEAD_DIM] + of[2 * HEAD_DIM:3 * HEAD_DIM] + of[3 * HEAD_DIM:]
        o_ref[pl.ds(pl.multiple_of(r * GRID_W, GRID_W), GRID_W), :] = o.astype(BF16)
        return carry

    lax.fori_loop(0, rows, one_row, 0)


def _na_bias_table(rpb):
    qc = np.arange(GRID_W)[:, None]
    kc = np.arange(GRID_W)[None, :]
    win = np.clip(qc - NA_KW // 2, 0, GRID_W - NA_KW)
    in_win = (kc >= win) & (kc < win + NA_KW)
    col_off = np.clip(kc - qc + NA_KW - 1, 0, 2 * NA_KW - 2)
    row_off = np.arange(NA_KH)[:, None] + np.arange(NA_KH)[None, :]
    b = rpb.astype(F32)[:, row_off][..., col_off]
    b = jnp.where(in_win[None, None, None], b, NEG_INF)
    b = b.transpose(1, 0, 3, 2, 4)
    return b.reshape(NA_KH, N_HEADS * GRID_W, NA_KH * GRID_W)


def _na(u_na, bias, batch, seq):
    rows = seq // GRID_W
    return pl.pallas_call(
        functools.partial(_na_body, rows=rows),
        grid=(batch,),
        in_specs=[
            pl.BlockSpec((seq, BRANCH_WIDTH), lambda b: (b, 0)),
            pl.BlockSpec((seq, BRANCH_WIDTH), lambda b: (b, 1)),
            pl.BlockSpec((seq, BRANCH_WIDTH), lambda b: (b, 2)),
            _const_spec(bias.shape),
        ],
        out_specs=pl.BlockSpec((seq, BRANCH_WIDTH), lambda b: (b, 0)),
        out_shape=jax.ShapeDtypeStruct((batch * seq, BRANCH_WIDTH), BF16),
        compiler_params=_cparams(("parallel",), 48),
        name="na",
    )(u_na, u_na, u_na, bias)


def _expand_heads(cols):
    hid = lax.broadcasted_iota(jnp.int32, (ML_CHUNK, BRANCH_WIDTH), 1) // HEAD_DIM
    out = jnp.broadcast_to(cols[:, 0:1], (ML_CHUNK, BRANCH_WIDTH))
    for h in range(1, N_HEADS):
        out = jnp.where(hid == h, cols[:, h:h + 1], out)
    return out


def _head_rowmax(x):
    hid = lax.broadcasted_iota(jnp.int32, x.shape, 1) // HEAD_DIM
    out = jnp.zeros_like(x)
    for h in range(N_HEADS):
        mh = jnp.max(jnp.where(hid == h, x, NEG_INF), axis=1, keepdims=True)
        out = jnp.where(hid == h, mh, out)
    return out


def _mlstm_direction(reverse, q_ref, k_ref, v_ref, gates_ref, gb_ref, hsum_ref, state_ref, n_chunks):
    L = ML_CHUNK
    W = BRANCH_WIDTH
    sub = lax.broadcasted_iota(jnp.int32, (L, W), 0)
    pos = lax.broadcasted_iota(jnp.int32, (L, W), 1) % HEAD_DIM
    diag = sub == pos
    causal = (pos >= sub) if reverse else (pos <= sub)
    tr = lax.broadcasted_iota(jnp.int32, (L, L), 0)
    tc = lax.broadcasted_iota(jnp.int32, (L, L), 1)
    tri = ((tc >= tr) if reverse else (tc <= tr)).astype(F32)
    ones_bd = _block_diag_mask(W, W).astype(BF16)
    bd2 = _block_diag_mask(W, 2 * W)
    i_off, f_off = (2 * N_HEADS, 3 * N_HEADS) if reverse else (0, N_HEADS)
    state_ref[...] = jnp.zeros_like(state_ref)

    def step(ci, m_run):
        c = (n_chunks - 1 - ci) if reverse else ci
        sl = pl.ds(pl.multiple_of(c * L, L), L)
        q = q_ref[sl, :]
        k = k_ref[sl, :]
        v = v_ref[sl, :]
        g = gates_ref[sl, :] + gb_ref[...]
        i_e = _expand_heads(g[:, i_off:i_off + N_HEADS])
        f_e = _expand_heads(jax.nn.log_sigmoid(g[:, f_off:f_off + N_HEADS]))
        b = jnp.dot(tri, f_e, precision=lax.Precision.HIGHEST, preferred_element_type=F32)
        b_end = b[0:1, :] if reverse else b[L - 1:L, :]
        rrow = jnp.sum(jnp.where(diag, i_e - b, 0.0), axis=0, keepdims=True)
        log_d = jnp.where(causal, b + rrow, NEG_INF)
        inter = b + m_run
        m_row = jnp.maximum(inter, _head_rowmax(log_d))
        d = jnp.exp(log_d - m_row)
        w_inter = jnp.exp(inter - m_row)
        s = lax.dot_general(q, _tile_heads(k), (((1,), (1,)), ((), ())), preferred_element_type=F32)
        p = (s * d).astype(BF16)
        r_loc = jnp.dot(p, jnp.concatenate([_tile_heads(v), ones_bd], axis=1), preferred_element_type=F32)
        r_int = jnp.dot(q, state_ref[...].astype(BF16), preferred_element_type=F32)
        num = r_loc[:, :W] + w_inter * r_int[:, :W]
        den = r_loc[:, W:] + w_inter * r_int[:, W:]
        h = num / jnp.maximum(jnp.abs(den), jnp.exp(-m_row))
        if reverse:
            hsum_ref[sl, :] = hsum_ref[sl, :] + h
        else:
            hsum_ref[sl, :] = h
        log_w = b_end - b + i_e
        m_new = jnp.maximum(b_end + m_run, jnp.max(log_w, axis=0, keepdims=True))
        w = jnp.exp(log_w - m_new)
        decay = jnp.exp(b_end + m_run - m_new)
        upd_rhs = jnp.concatenate([(w * v.astype(F32)).astype(BF16), w.astype(BF16)], axis=1)
        upd = lax.dot_general(k, upd_rhs, (((0,), (0,)), ((), ())), preferred_element_type=F32)
        upd = jnp.where(bd2, upd, 0.0)
        state_ref[...] = jnp.concatenate([decay, decay], axis=1) * state_ref[...] + upd
        return m_new

    lax.fori_loop(0, n_chunks, step, jnp.full((1, W), NEG_INF, F32))


def _mlstm_body(q_ref, k_ref, v_ref, og_ref, gates_ref, gb_ref, ng_ref, o_ref, hsum_ref, state_ref, *, seq):
    n_chunks = seq // ML_CHUNK
    for reverse in (False, True):
        _mlstm_direction(reverse, q_ref, k_ref, v_ref, gates_ref, gb_ref, hsum_ref, state_ref, n_chunks)
    ones_bd = _block_diag_mask(BRANCH_WIDTH, BRANCH_WIDTH).astype(BF16)
    tile = 256

    def finish(t, carry):
        sl = pl.ds(pl.multiple_of(t * tile, tile), tile)
        hs = hsum_ref[sl, :]
        ms = jnp.dot((hs * hs).astype(BF16), ones_bd, preferred_element_type=F32) * (1.0 / HEAD_DIM)
        hm = hs * lax.rsqrt(ms + NORM_EPS) * ng_ref[...]
        o_ref[sl, :] = (hm * jax.nn.sigmoid(og_ref[sl, :].astype(F32))).astype(BF16)
        return carry

    lax.fori_loop(0, seq // tile, finish, 0)


def _mlstm(u_ml, gates, gate_b, norm_g, batch, seq):
    col = lambda j: pl.BlockSpec((seq, BRANCH_WIDTH), lambda b: (b, j))
    return pl.pallas_call(
        functools.partial(_mlstm_body, seq=seq),
        grid=(batch,),
        in_specs=[col(0), col(1), col(2), col(3),
                  pl.BlockSpec((seq, LANES), lambda b: (b, 0)),
                  _const_spec((1, LANES)), _const_spec((1, BRANCH_WIDTH))],
        out_specs=pl.BlockSpec((seq, BRANCH_WIDTH), lambda b: (b, 0)),
        out_shape=jax.ShapeDtypeStruct((batch * seq, BRANCH_WIDTH), BF16),
        scratch_shapes=[pltpu.VMEM((seq, BRANCH_WIDTH), F32),
                        pltpu.VMEM((BRANCH_WIDTH, 2 * BRANCH_WIDTH), F32)],
        compiler_params=_cparams(("parallel",), 48),
        name="mlstm",
    )(u_ml, u_ml, u_ml, u_ml, gates, gate_b, norm_g)


CONV_PAD = 16
CONV_TILE = 64


def _conv_body(u_ref, w_ref, b_ref, g_ref, beta_ref, o_ref, pad_ref, *, seq):
    u = u_ref[...]
    a = u[:, :BRANCH_WIDTH].astype(F32)
    gate = u[:, BRANCH_WIDTH:].astype(F32)
    zeros = jnp.zeros((CONV_PAD, BRANCH_WIDTH), F32)
    pad_ref[0:CONV_PAD, :] = zeros
    pad_ref[CONV_PAD + seq:, :] = zeros
    pad_ref[CONV_PAD:CONV_PAD + seq, :] = a * jax.nn.sigmoid(gate)
    first = CONV_PAD - CONV_WIDTH // 2

    def tile(t, carry):
        base = pl.multiple_of(t * CONV_TILE, CONV_TILE)
        win = pad_ref[pl.ds(base, CONV_TILE + 2 * CONV_PAD), :]
        acc = jnp.zeros((CONV_TILE, BRANCH_WIDTH), F32)
        for j in range(CONV_WIDTH):
            acc = acc + win[first + j:first + j + CONV_TILE, :] * w_ref[j:j + 1, :]
        y = acc + b_ref[...]
        mu = jnp.mean(y, axis=-1, keepdims=True)
        yc = y - mu
        var = jnp.mean(yc * yc, axis=-1, keepdims=True)
        z = yc * lax.rsqrt(var + NORM_EPS) * g_ref[...] + beta_ref[...]
        o_ref[pl.ds(base, CONV_TILE), :] = (z * jax.nn.sigmoid(z)).astype(BF16)
        return carry

    lax.fori_loop(0, seq // CONV_TILE, tile, 0)


def _conv(u_cv, w, b, g, beta, batch, seq):
    return pl.pallas_call(
        functools.partial(_conv_body, seq=seq),
        grid=(batch,),
        in_specs=[pl.BlockSpec((seq, CONV_COLS), lambda i: (i, 0)),
                  _const_spec(w.shape), _const_spec(b.shape), _const_spec(g.shape), _const_spec(beta.shape)],
        out_specs=pl.BlockSpec((seq, BRANCH_WIDTH), lambda i: (i, 0)),
        out_shape=jax.ShapeDtypeStruct((batch * seq, BRANCH_WIDTH), BF16),
        scratch_shapes=[pltpu.VMEM((seq + 2 * CONV_PAD, BRANCH_WIDTH), F32)],
        compiler_params=_cparams(("parallel",), 32),
        name="conv",
    )(u_cv, w, b, g, beta)


ATT_Q_TILE = 256


def _gqa_body(q_ref, k_ref, v_ref, o_ref):
    k = k_ref[...]
    v = v_ref[...]
    lane = lax.broadcasted_iota(jnp.int32, (ATT_Q_TILE, LANES), 1)
    low = lane < HEAD_DIM
    for c in range(2):
        qc = q_ref[:, c * LANES:(c + 1) * LANES]
        halves = []
        for keep in (low, jnp.logical_not(low)):
            qh = jnp.where(keep, qc, jnp.zeros_like(qc))
            s = lax.dot_general(qh, k, (((1,), (1,)), ((), ())), preferred_element_type=F32)
            m = jnp.max(s, axis=1, keepdims=True)
            p = jnp.exp(s - m)
            l = jnp.sum(p, axis=1, keepdims=True)
            halves.append(jnp.dot(p.astype(BF16), v, preferred_element_type=F32) / l)
        o_ref[:, c * LANES:(c + 1) * LANES] = jnp.where(low, halves[0], halves[1]).astype(BF16)


def _gqa(qk, v, batch, seq):
    tiles = seq // ATT_Q_TILE
    return pl.pallas_call(
        _gqa_body,
        grid=(batch, tiles),
        in_specs=[pl.BlockSpec((ATT_Q_TILE, N_HEADS * HEAD_DIM), lambda b, i: (b * tiles + i, 0)),
                  pl.BlockSpec((seq, LANES), lambda b, i: (b, 2)),
                  pl.BlockSpec((seq, LANES), lambda b, i: (b, 0))],
        out_specs=pl.BlockSpec((ATT_Q_TILE, BRANCH_WIDTH), lambda b, i: (b * tiles + i, 0)),
        out_shape=jax.ShapeDtypeStruct((batch * seq, BRANCH_WIDTH), BF16),
        compiler_params=_cparams(("parallel", "parallel"), 48),
        name="gqa",
    )(qk, qk, v)


def _merge_body(x_ref, g_ref, b0_ref, b1_ref, b2_ref, b3_ref, wg_ref, bg_ref, wb_ref, wo_ref, o_ref):
    x = x_ref[...]
    h = _rms(x, g_ref[...]).astype(BF16)
    acc = jnp.zeros(x.shape, F32)
    for n, br_ref in enumerate((b0_ref, b1_ref, b2_ref, b3_ref)):
        cols = slice(n * D_MODEL, (n + 1) * D_MODEL)
        gl = jnp.dot(h, wg_ref[:, cols], preferred_element_type=F32) + bg_ref[:, cols]
        pr = jnp.dot(br_ref[...], wb_ref[n], preferred_element_type=F32)
        acc = acc + jax.nn.sigmoid(gl) * pr
    o_ref[...] = x + jnp.dot(acc.astype(BF16), wo_ref[...], preferred_element_type=F32)


def _merge(x, g, branches, w_gate, b_gate, w_branch, w_out):
    n_tok = x.shape[0]
    tm = TOKEN_TILE
    row = lambda i: (i, 0)
    return pl.pallas_call(
        _merge_body,
        grid=(n_tok // tm,),
        in_specs=[pl.BlockSpec((tm, D_MODEL), row), _const_spec((1, D_MODEL))]
        + [pl.BlockSpec((tm, BRANCH_WIDTH), row)] * N_BRANCHES
        + [_const_spec(w_gate.shape), _const_spec(b_gate.shape), _const_spec(w_branch.shape), _const_spec(w_out.shape)],
        out_specs=pl.BlockSpec((tm, D_MODEL), row),
        out_shape=jax.ShapeDtypeStruct((n_tok, D_MODEL), F32),
        compiler_params=_cparams(("parallel",), 56),
        name="merge",
    )(x, g, *branches, w_gate, b_gate, w_branch, w_out)


MOE_TOKEN_TILE = 1024


def _router(h, wr_ref, br_ref):
    logits = jnp.dot(h, wr_ref[...], precision=lax.Precision.HIGHEST, preferred_element_type=F32) + br_ref[...]
    lane = lax.broadcasted_iota(jnp.int32, logits.shape, 1)
    big = jnp.int32(ROUTER_LANES)
    is_g = lane < MOE_GROUPS
    gl = jnp.where(is_g, logits, NEG_INF)
    g_max = jnp.max(gl, axis=1, keepdims=True)
    g_idx = jnp.min(jnp.where(is_g & (gl == g_max), lane, big), axis=1, keepdims=True)
    g_gate = 1.0 / jnp.sum(jnp.where(is_g, jnp.exp(gl - g_max), 0.0), axis=1, keepdims=True)
    e_lo = ROUTER_E0 + g_idx * MOE_EXPERTS_PER_GROUP
    in_grp = (lane >= e_lo) & (lane < e_lo + MOE_EXPERTS_PER_GROUP)
    el = jnp.where(in_grp, logits, NEG_INF)
    top1 = jnp.max(el, axis=1, keepdims=True)
    idx1 = jnp.min(jnp.where(in_grp & (el == top1), lane, big), axis=1, keepdims=True)
    rest = in_grp & (lane != idx1)
    el2 = jnp.where(rest, logits, NEG_INF)
    top2 = jnp.max(el2, axis=1, keepdims=True)
    idx2 = jnp.min(jnp.where(rest & (el2 == top2), lane, big), axis=1, keepdims=True)
    e2 = jnp.exp(top2 - top1)
    w1 = g_gate / (1.0 + e2)
    w2 = g_gate * e2 / (1.0 + e2)
    return jnp.where(lane == idx1, w1, 0.0) + jnp.where(lane == idx2, w2, 0.0)


def _moe_body(x_ref, g_ref, wr_ref, br_ref, w1_ref, w3_ref, w2_ref, fg_ref, o_ref, h_ref, comb_ref, acc_ref, *, final):
    e = pl.program_id(1)

    @pl.when(e == 0)
    def _():
        h = _rms(x_ref[...], g_ref[...])
        h_ref[...] = h.astype(BF16)
        comb_ref[...] = _router(h, wr_ref, br_ref)
        acc_ref[...] = jnp.zeros_like(acc_ref)

    h = h_ref[...]
    a = jnp.dot(h, w1_ref[0], preferred_element_type=F32)
    b = jnp.dot(h, w3_ref[0], preferred_element_type=F32)
    comb = comb_ref[...]
    lane = lax.broadcasted_iota(jnp.int32, comb.shape, 1)
    c = jnp.sum(jnp.where(lane == ROUTER_E0 + e, comb, 0.0), axis=1, keepdims=True)
    hid = (a * jax.nn.sigmoid(a) * b * c).astype(BF16)
    acc_ref[...] += jnp.dot(hid, w2_ref[0], preferred_element_type=F32)

    @pl.when(e == MOE_N_EXPERTS - 1)
    def _():
        y = x_ref[...] + acc_ref[...]
        o_ref[...] = _rms(y, fg_ref[...]) if final else y


def _moe(x, g, wr, br, w1, w3, w2, final_g, final):
    n_tok = x.shape[0]
    tm = MOE_TOKEN_TILE
    row = lambda i, e: (i, 0)
    exp = lambda i, e: (e, 0, 0)
    return pl.pallas_call(
        functools.partial(_moe_body, final=final),
        grid=(n_tok // tm, MOE_N_EXPERTS),
        in_specs=[pl.BlockSpec((tm, D_MODEL), row), _const_spec((1, D_MODEL)),
                  _const_spec(wr.shape), _const_spec(br.shape),
                  pl.BlockSpec((1, D_MODEL, MOE_D_EXPERT), exp), pl.BlockSpec((1, D_MODEL, MOE_D_EXPERT), exp),
                  pl.BlockSpec((1, MOE_D_EXPERT, D_MODEL), exp), _const_spec((1, D_MODEL))],
        out_specs=pl.BlockSpec((tm, D_MODEL), row),
        out_shape=jax.ShapeDtypeStruct((n_tok, D_MODEL), F32),
        scratch_shapes=[pltpu.VMEM((tm, D_MODEL), BF16), pltpu.VMEM((tm, ROUTER_LANES), F32),
                        pltpu.VMEM((tm, D_MODEL), F32)],
        compiler_params=_cparams(("parallel", "arbitrary"), 48),
        name="moe",
    )(x, g, wr, br, w1, w3, w2, final_g)


def _rope_tables(seq):
    t = jnp.arange(seq)
    pos = jnp.stack([t // GRID_W, t % GRID_W], axis=-1).astype(F32)
    inv = ROPE_THETA ** (-jnp.arange(0, ROPE_AXIS_DIM, 2, dtype=F32) / ROPE_AXIS_DIM)
    ang = pos[..., None] * inv
    ang = jnp.concatenate([ang, ang], axis=-1).reshape(seq, HEAD_DIM)
    first = (jnp.arange(HEAD_DIM) % ROPE_AXIS_DIM) < (ROPE_AXIS_DIM // 2)
    cos = jnp.cos(ang)
    sin = jnp.where(first[None, :], -jnp.sin(ang), jnp.sin(ang))
    reps = QK_COLS // HEAD_DIM
    return jnp.tile(cos, (1, reps)), jnp.tile(sin, (1, reps))


ATT_HEAD_ORDER = (0, 2, 1, 3)


def _pack_layer(l, w_in, na_rpb, ml_gate_b, ml_norm_g, conv_dw_w, conv_dw_b, conv_norm_g, conv_norm_b,
                att_q_norm_g, att_k_norm_g, w_branch, w_gate, b_gate, w_out, moe_w_group, moe_b_group,
                moe_w_expert, moe_b_expert, moe_w1, moe_w3, moe_w2):
    scale = HEAD_DIM ** -0.5
    w = w_in[l]
    o0 = 0
    wna = w[:, o0:o0 + NA_COLS]
    wna = jnp.concatenate([wna[:, :BRANCH_WIDTH] * scale, wna[:, BRANCH_WIDTH:]], axis=1)
    o0 += NA_COLS
    wml = w[:, o0:o0 + 4 * BRANCH_WIDTH]
    wml = jnp.concatenate([wml[:, :BRANCH_WIDTH], wml[:, BRANCH_WIDTH:2 * BRANCH_WIDTH] * scale,
                           wml[:, 2 * BRANCH_WIDTH:]], axis=1)
    o0 += 4 * BRANCH_WIDTH
    n_gates = ML_N_GATES * N_HEADS
    wg = jnp.pad(w[:, o0:o0 + n_gates], ((0, 0), (0, LANES - n_gates)))
    o0 += n_gates
    wcv = w[:, o0:o0 + CONV_COLS]
    o0 += CONV_COLS
    wat = w[:, o0:o0 + ATT_COLS]
    wq = wat[:, :N_HEADS * HEAD_DIM].reshape(D_MODEL, N_HEADS, HEAD_DIM)[:, ATT_HEAD_ORDER, :]
    wat = jnp.concatenate([wq.reshape(D_MODEL, N_HEADS * HEAD_DIM), wat[:, N_HEADS * HEAD_DIM:]], axis=1)
    qkg = jnp.concatenate([jnp.tile(att_q_norm_g[l] * scale, N_HEADS), jnp.tile(att_k_norm_g[l], ATT_KV_HEADS)])
    wb = w_branch[l]
    wb3 = wb[3].reshape(N_HEADS, HEAD_DIM, D_MODEL)[ATT_HEAD_ORDER, :, :].reshape(BRANCH_WIDTH, D_MODEL)
    wb = jnp.concatenate([wb[:3], wb3[None]], axis=0)
    wr = jnp.concatenate([moe_w_group[l], moe_w_expert[l].reshape(D_MODEL, MOE_N_EXPERTS)], axis=1)
    n_r = MOE_GROUPS + MOE_N_EXPERTS
    br = jnp.concatenate([moe_b_group[l], moe_b_expert[l].reshape(MOE_N_EXPERTS)])
    return dict(
        wna=wna.astype(BF16), wml=wml.astype(BF16), wg=wg.astype(BF16), wcv=wcv.astype(BF16), wat=wat.astype(BF16),
        qkg=qkg.reshape(1, QK_COLS).astype(F32),
        na_bias=_na_bias_table(na_rpb[l]),
        gate_b=jnp.pad(ml_gate_b[l].reshape(1, n_gates), ((0, 0), (0, LANES - n_gates))).astype(F32),
        ml_norm_g=ml_norm_g[l].reshape(1, BRANCH_WIDTH),
        conv_w=conv_dw_w[l], conv_b=conv_dw_b[l].reshape(1, BRANCH_WIDTH),
        conv_g=conv_norm_g[l].reshape(1, BRANCH_WIDTH), conv_beta=conv_norm_b[l].reshape(1, BRANCH_WIDTH),
        w_gate=w_gate[l].astype(BF16), b_gate=b_gate[l].reshape(1, N_BRANCHES * D_MODEL),
        w_branch=wb.astype(BF16), w_out=w_out[l].astype(BF16),
        wr=jnp.pad(wr, ((0, 0), (0, ROUTER_LANES - n_r))), br=jnp.pad(br, (0, ROUTER_LANES - n_r)).reshape(1, ROUTER_LANES),
        w1=moe_w1[l].astype(BF16), w3=moe_w3[l].astype(BF16), w2=moe_w2[l].astype(BF16),
    )


def _trunk(x, layers, norm_mix_g, norm_ffn_g, final_norm_g, ones_qk, depth):
    batch, seq, _ = x.shape
    xf = x.reshape(batch * seq, D_MODEL)
    cos_t, sin_t = _rope_tables(seq)
    fg = final_norm_g.reshape(1, D_MODEL)
    for l in range(depth):
        p = layers[l]
        gm = norm_mix_g[l].reshape(1, D_MODEL)
        u_na, u_ml, gates, u_cv, qk, v = _inproj(xf, gm, p["wna"], p["wml"], p["wg"], p["wcv"], p["wat"], ones_qk,
                                                 p["qkg"], cos_t, sin_t, seq)
        o_na = _na(u_na, p["na_bias"], batch, seq)
        o_ml = _mlstm(u_ml, gates, p["gate_b"], p["ml_norm_g"], batch, seq)
        o_cv = _conv(u_cv, p["conv_w"], p["conv_b"], p["conv_g"], p["conv_beta"], batch, seq)
        o_at = _gqa(qk, v, batch, seq)
        xf = _merge(xf, gm, (o_na, o_ml, o_cv, o_at), p["w_gate"], p["b_gate"], p["w_branch"], p["w_out"])
        xf = _moe(xf, norm_ffn_g[l].reshape(1, D_MODEL), p["wr"], p["br"], p["w1"], p["w3"], p["w2"], fg,
                  final=(l == depth - 1))
    return xf.reshape(batch, seq, D_MODEL)


def kernel(x_prompt, x_sample, norm_mix_g, w_in, na_rpb, ml_gate_b, ml_norm_g, conv_dw_w, conv_dw_b, conv_norm_g, conv_norm_b, att_q_norm_g, att_k_norm_g, w_branch, w_gate, b_gate, w_out, norm_ffn_g, moe_w_group, moe_b_group, moe_w_expert, moe_b_expert, moe_w1, moe_w3, moe_w2, final_norm_g):
    depth = w_in.shape[0]
    layers = [_pack_layer(l, w_in, na_rpb, ml_gate_b, ml_norm_g, conv_dw_w, conv_dw_b, conv_norm_g, conv_norm_b,
                          att_q_norm_g, att_k_norm_g, w_branch, w_gate, b_gate, w_out, moe_w_group, moe_b_group,
                          moe_w_expert, moe_b_expert, moe_w1, moe_w3, moe_w2) for l in range(depth)]
    hid = np.arange(QK_COLS) // HEAD_DIM
    ones_qk = jnp.asarray(hid[:, None] == hid[None, :], BF16)
    y_prompt = _trunk(x_prompt, layers, norm_mix_g, norm_ffn_g, final_norm_g, ones_qk, depth)
    y_sample = _trunk(x_sample, layers, norm_mix_g, norm_ffn_g, final_norm_g, ones_qk, depth)
    return (y_prompt, y_sample)
```

```python
import functools

import jax
import jax.numpy as jnp
import numpy as np
from jax import lax
from jax.experimental import pallas as pl
from jax.experimental.pallas import tpu as pltpu

F32 = jnp.float32
BF16 = jnp.bfloat16

D_MODEL = 1024
GRID_W = 64
HEAD_DIM = 64
BRANCH_WIDTH = 256
N_BRANCHES = 4
N_HEADS = 4
NA_KH = 8
NA_KW = 16
ML_CHUNK = 64
ML_N_GATES = 4
CONV_WIDTH = 31
ATT_KV_HEADS = 2
ROPE_THETA = 10000.0
ROPE_AXIS_DIM = HEAD_DIM // 2
MOE_GROUPS = 4
MOE_EXPERTS_PER_GROUP = 4
MOE_N_EXPERTS = 16
MOE_D_EXPERT = 256
NORM_EPS = 1e-6
NEG_INF = -1e30

NA_COLS = 3 * BRANCH_WIDTH
ML_COLS = 4 * BRANCH_WIDTH + ML_N_GATES * N_HEADS
CONV_COLS = 2 * BRANCH_WIDTH
ATT_COLS = (N_HEADS + 2 * ATT_KV_HEADS) * HEAD_DIM
QK_COLS = (N_HEADS + ATT_KV_HEADS) * HEAD_DIM

LANES = 128
SUBLANES = 8
ROUTER_LANES = LANES
ROUTER_E0 = MOE_GROUPS
V7X_VMEM_BYTES = 64 * 1024 * 1024

TOKEN_TILE = 512


def _cparams(semantics, vmem_mb):
    assert vmem_mb * 1024 * 1024 < V7X_VMEM_BYTES
    return pltpu.CompilerParams(dimension_semantics=semantics, vmem_limit_bytes=vmem_mb * 1024 * 1024)


def _const_spec(shape):
    nd = len(shape)
    return pl.BlockSpec(shape, lambda *_: (0,) * nd)


def _rms(x, g):
    ms = jnp.mean(x * x, axis=-1, keepdims=True)
    return x * lax.rsqrt(ms + NORM_EPS) * g


def _inproj_body(x_ref, g_ref, wna_ref, wml_ref, wg_ref, wcv_ref, wat_ref, ones_ref, qkg_ref, cos_ref, sin_ref,
                 una_ref, uml_ref, gates_ref, ucv_ref, qk_ref, v_ref):
    h = _rms(x_ref[...], g_ref[...]).astype(BF16)
    una_ref[...] = jnp.dot(h, wna_ref[...], preferred_element_type=F32).astype(BF16)
    uml_ref[...] = jnp.dot(h, wml_ref[...], preferred_element_type=F32).astype(BF16)
    gates_ref[...] = jnp.dot(h, wg_ref[...], preferred_element_type=F32)
    ucv_ref[...] = jnp.dot(h, wcv_ref[...], preferred_element_type=F32).astype(BF16)
    ua = jnp.dot(h, wat_ref[...], preferred_element_type=F32)
    qk = ua[:, :QK_COLS]
    ms = jnp.dot((qk * qk).astype(BF16), ones_ref[...], preferred_element_type=F32) * (1.0 / HEAD_DIM)
    qn = qk * lax.rsqrt(ms + NORM_EPS) * qkg_ref[...]
    half = ROPE_AXIS_DIM // 2
    rot = []
    for c in range(QK_COLS // LANES):
        xc = qn[:, c * LANES:(c + 1) * LANES]
        lane = lax.broadcasted_iota(jnp.int32, xc.shape, 1)
        rot.append(jnp.where((lane % ROPE_AXIS_DIM) < half, pltpu.roll(xc, LANES - half, 1), pltpu.roll(xc, half, 1)))
    rot = jnp.concatenate(rot, axis=1)
    qk_ref[...] = (qn * cos_ref[...] + rot * sin_ref[...]).astype(BF16)
    v_ref[...] = ua[:, QK_COLS:].astype(BF16)


def _inproj(x, g, wna, wml, wg, wcv, wat, ones_qk, qkg, cos_t, sin_t, seq):
    n_tok = x.shape[0]
    tm = TOKEN_TILE
    tiles_per_seq = seq // tm
    row = lambda i: (i, 0)
    pos = lambda i: (i % tiles_per_seq, 0)
    outs = [
        jax.ShapeDtypeStruct((n_tok, NA_COLS), BF16),
        jax.ShapeDtypeStruct((n_tok, 4 * BRANCH_WIDTH), BF16),
        jax.ShapeDtypeStruct((n_tok, LANES), F32),
        jax.ShapeDtypeStruct((n_tok, CONV_COLS), BF16),
        jax.ShapeDtypeStruct((n_tok, QK_COLS), BF16),
        jax.ShapeDtypeStruct((n_tok, ATT_KV_HEADS * HEAD_DIM), BF16),
    ]
    return pl.pallas_call(
        _inproj_body,
        grid=(n_tok // tm,),
        in_specs=[
            pl.BlockSpec((tm, D_MODEL), row),
            _const_spec((1, D_MODEL)),
            _const_spec(wna.shape), _const_spec(wml.shape), _const_spec(wg.shape), _const_spec(wcv.shape),
            _const_spec(wat.shape), _const_spec(ones_qk.shape), _const_spec(qkg.shape),
            pl.BlockSpec((tm, QK_COLS), pos), pl.BlockSpec((tm, QK_COLS), pos),
        ],
        out_specs=[pl.BlockSpec((tm, o.shape[1]), row) for o in outs],
        out_shape=outs,
        compiler_params=_cparams(("parallel",), 48),
        name="inproj",
    )(x, g, wna, wml, wg, wcv, wat, ones_qk, qkg, cos_t, sin_t)


def _block_diag_mask(rows, cols):
    r = lax.broadcasted_iota(jnp.int32, (rows, cols), 0) // HEAD_DIM
    c = lax.broadcasted_iota(jnp.int32, (rows, cols), 1) // HEAD_DIM
    return r == (c % N_HEADS)


def _tile_heads(x):
    t = jnp.concatenate([x] * N_HEADS, axis=0)
    return jnp.where(_block_diag_mask(N_HEADS * HEAD_DIM, BRANCH_WIDTH), t, jnp.zeros_like(t))


def _na_body(q_ref, k_ref, v_ref, bias_ref, o_ref, *, rows):
    n_keys = NA_KH * GRID_W

    def one_row(r, carry):
        r0 = jnp.clip(r - NA_KH // 2, 0, rows - NA_KH)
        delta = r0 - r + NA_KH - 1
        q = q_ref[pl.ds(pl.multiple_of(r * GRID_W, GRID_W), GRID_W), :]
        kk = k_ref[pl.ds(pl.multiple_of(r0 * GRID_W, GRID_W), n_keys), :]
        vv = v_ref[pl.ds(pl.multiple_of(r0 * GRID_W, GRID_W), n_keys), :]
        s = lax.dot_general(_tile_heads(q), kk, (((1,), (1,)), ((), ())), preferred_element_type=F32)
        s = s + bias_ref[delta]
        m = jnp.max(s, axis=1, keepdims=True)
        p = jnp.exp(s - m)
        l = jnp.sum(p, axis=1, keepdims=True)
        of = jnp.dot(p.astype(BF16), vv, preferred_element_type=F32) / l
        of = jnp.where(_block_diag_mask(N_HEADS * HEAD_DIM, BRANCH_WIDTH), of, 0.0)
        o = of[0:HEAD_DIM] + of[HEAD_DIM:2 * HEAD_DIM] + of[2 * HEAD_DIM:3 * HEAD_DIM] + of[3 * HEAD_DIM:]
        o_ref[pl.ds(pl.multiple_of(r * GRID_W, GRID_W), GRID_W), :] = o.astype(BF16)
        return carry

    lax.fori_loop(0, rows, one_row, 0)


def _na_bias_table(rpb):
    qc = np.arange(GRID_W)[:, None]
    kc = np.arange(GRID_W)[None, :]
    win = np.clip(qc - NA_KW // 2, 0, GRID_W - NA_KW)
    in_win = (kc >= win) & (kc < win + NA_KW)
    col_off = np.clip(kc - qc + NA_KW - 1, 0, 2 * NA_KW - 2)
    n_ro, n_co = 2 * NA_KH - 1, 2 * NA_KW - 1
    pick = np.zeros((n_co, GRID_W * GRID_W), np.float32)
    pick[col_off.reshape(-1), np.arange(GRID_W * GRID_W)] = 1.0
    c = jnp.dot(rpb.astype(F32).reshape(N_HEADS * n_ro, n_co), pick, precision=lax.Precision.HIGHEST)
    c = jnp.where(in_win.reshape(1, -1), c, NEG_INF).reshape(N_HEADS, n_ro, GRID_W, GRID_W)
    c = c.transpose(0, 2, 1, 3)
    b = jnp.stack([c[:, :, d:d + NA_KH, :] for d in range(NA_KH)], axis=0)
    return b.reshape(NA_KH, N_HEADS * GRID_W, NA_KH * GRID_W)


def _na(u_na, bias, batch, seq):
    rows = seq // GRID_W
    return pl.pallas_call(
        functools.partial(_na_body, rows=rows),
        grid=(batch,),
        in_specs=[
            pl.BlockSpec((seq, BRANCH_WIDTH), lambda b: (b, 0)),
            pl.BlockSpec((seq, BRANCH_WIDTH), lambda b: (b, 1)),
            pl.BlockSpec((seq, BRANCH_WIDTH), lambda b: (b, 2)),
            _const_spec(bias.shape),
        ],
        out_specs=pl.BlockSpec((seq, BRANCH_WIDTH), lambda b: (b, 0)),
        out_shape=jax.ShapeDtypeStruct((batch * seq, BRANCH_WIDTH), BF16),
        compiler_params=_cparams(("parallel",), 48),
        name="na",
    )(u_na, u_na, u_na, bias)


def _expand_heads(cols):
    hid = lax.broadcasted_iota(jnp.int32, (ML_CHUNK, BRANCH_WIDTH), 1) // HEAD_DIM
    out = jnp.broadcast_to(cols[:, 0:1], (ML_CHUNK, BRANCH_WIDTH))
    for h in range(1, N_HEADS):
        out = jnp.where(hid == h, cols[:, h:h + 1], out)
    return out


def _head_rowmax(x):
    hid = lax.broadcasted_iota(jnp.int32, x.shape, 1) // HEAD_DIM
    out = jnp.zeros_like(x)
    for h in range(N_HEADS):
        mh = jnp.max(jnp.where(hid == h, x, NEG_INF), axis=1, keepdims=True)
        out = jnp.where(hid == h, mh, out)
    return out


def _mlstm_step_fn(reverse, q_ref, k_ref, v_ref, gates_ref, gb_ref, h_ref, state_ref, n_chunks):
    L = ML_CHUNK
    W = BRANCH_WIDTH
    sub = lax.broadcasted_iota(jnp.int32, (L, W), 0)
    pos = lax.broadcasted_iota(jnp.int32, (L, W), 1) % HEAD_DIM
    diag = sub == pos
    causal = (pos >= sub) if reverse else (pos <= sub)
    tr = lax.broadcasted_iota(jnp.int32, (L, L), 0)
    tc = lax.broadcasted_iota(jnp.int32, (L, L), 1)
    tri = ((tc >= tr) if reverse else (tc <= tr)).astype(F32)
    ones_bd = _block_diag_mask(W, W).astype(BF16)
    bd2 = _block_diag_mask(W, 2 * W)
    i_off, f_off = (2 * N_HEADS, 3 * N_HEADS) if reverse else (0, N_HEADS)
    state_ref[...] = jnp.zeros_like(state_ref)

    def step(ci, m_run):
        c = (n_chunks - 1 - ci) if reverse else ci
        sl = pl.ds(pl.multiple_of(c * L, L), L)
        q = q_ref[sl, :]
        k = k_ref[sl, :]
        v = v_ref[sl, :]
        g = gates_ref[sl, :] + gb_ref[...]
        i_e = _expand_heads(g[:, i_off:i_off + N_HEADS])
        f_e = _expand_heads(jax.nn.log_sigmoid(g[:, f_off:f_off + N_HEADS]))
        b = jnp.dot(tri, f_e, precision=lax.Precision.HIGHEST, preferred_element_type=F32)
        b_end = b[0:1, :] if reverse else b[L - 1:L, :]
        rrow = jnp.sum(jnp.where(diag, i_e - b, 0.0), axis=0, keepdims=True)
        log_d = jnp.where(causal, b + rrow, NEG_INF)
        inter = b + m_run
        m_row = jnp.maximum(inter, _head_rowmax(log_d))
        d = jnp.exp(log_d - m_row)
        w_inter = jnp.exp(inter - m_row)
        s = lax.dot_general(q, _tile_heads(k), (((1,), (1,)), ((), ())), preferred_element_type=F32)
        p = (s * d).astype(BF16)
        r_loc = jnp.dot(p, jnp.concatenate([_tile_heads(v), ones_bd], axis=1), preferred_element_type=F32)
        r_int = jnp.dot(q, state_ref[...].astype(BF16), preferred_element_type=F32)
        num = r_loc[:, :W] + w_inter * r_int[:, :W]
        den = r_loc[:, W:] + w_inter * r_int[:, W:]
        h_ref[sl, :] = num / jnp.maximum(jnp.abs(den), jnp.exp(-m_row))
        log_w = b_end - b + i_e
        m_new = jnp.maximum(b_end + m_run, jnp.max(log_w, axis=0, keepdims=True))
        w = jnp.exp(log_w - m_new)
        decay = jnp.exp(b_end + m_run - m_new)
        upd_rhs = jnp.concatenate([(w * v.astype(F32)).astype(BF16), w.astype(BF16)], axis=1)
        upd = lax.dot_general(k, upd_rhs, (((0,), (0,)), ((), ())), preferred_element_type=F32)
        upd = jnp.where(bd2, upd, 0.0)
        state_ref[...] = jnp.concatenate([decay, decay], axis=1) * state_ref[...] + upd
        return m_new

    return step


def _mlstm_body(q_ref, k_ref, v_ref, og_ref, gates_ref, gb_ref, ng_ref, o_ref, hf_ref, hb_ref, sf_ref, sb_ref, *, seq):
    n_chunks = seq // ML_CHUNK
    step_f = _mlstm_step_fn(False, q_ref, k_ref, v_ref, gates_ref, gb_ref, hf_ref, sf_ref, n_chunks)
    step_b = _mlstm_step_fn(True, q_ref, k_ref, v_ref, gates_ref, gb_ref, hb_ref, sb_ref, n_chunks)
    m0 = jnp.full((1, BRANCH_WIDTH), NEG_INF, F32)
    lax.fori_loop(0, n_chunks, lambda ci, m: (step_f(ci, m[0]), step_b(ci, m[1])), (m0, m0))
    ones_bd = _block_diag_mask(BRANCH_WIDTH, BRANCH_WIDTH).astype(BF16)
    tile = 256

    def finish(t, carry):
        sl = pl.ds(pl.multiple_of(t * tile, tile), tile)
        hs = hf_ref[sl, :] + hb_ref[sl, :]
        ms = jnp.dot((hs * hs).astype(BF16), ones_bd, preferred_element_type=F32) * (1.0 / HEAD_DIM)
        hm = hs * lax.rsqrt(ms + NORM_EPS) * ng_ref[...]
        o_ref[sl, :] = (hm * jax.nn.sigmoid(og_ref[sl, :].astype(F32))).astype(BF16)
        return carry

    lax.fori_loop(0, seq // tile, finish, 0)


def _mlstm(u_ml, gates, gate_b, norm_g, batch, seq):
    col = lambda j: pl.BlockSpec((seq, BRANCH_WIDTH), lambda b: (b, j))
    return pl.pallas_call(
        functools.partial(_mlstm_body, seq=seq),
        grid=(batch,),
        in_specs=[col(0), col(1), col(2), col(3),
                  pl.BlockSpec((seq, LANES), lambda b: (b, 0)),
                  _const_spec((1, LANES)), _const_spec((1, BRANCH_WIDTH))],
        out_specs=pl.BlockSpec((seq, BRANCH_WIDTH), lambda b: (b, 0)),
        out_shape=jax.ShapeDtypeStruct((batch * seq, BRANCH_WIDTH), BF16),
        scratch_shapes=[pltpu.VMEM((seq, BRANCH_WIDTH), F32), pltpu.VMEM((seq, BRANCH_WIDTH), F32),
                        pltpu.VMEM((BRANCH_WIDTH, 2 * BRANCH_WIDTH), F32),
                        pltpu.VMEM((BRANCH_WIDTH, 2 * BRANCH_WIDTH), F32)],
        compiler_params=_cparams(("parallel",), 48),
        name="mlstm",
    )(u_ml, u_ml, u_ml, u_ml, gates, gate_b, norm_g)


CONV_PAD = 16
CONV_TILE = 64


def _conv_body(u_ref, w_ref, b_ref, g_ref, beta_ref, o_ref, pad_ref, *, seq):
    u = u_ref[...]
    a = u[:, :BRANCH_WIDTH].astype(F32)
    gate = u[:, BRANCH_WIDTH:].astype(F32)
    zeros = jnp.zeros((CONV_PAD, BRANCH_WIDTH), F32)
    pad_ref[0:CONV_PAD, :] = zeros
    pad_ref[CONV_PAD + seq:, :] = zeros
    pad_ref[CONV_PAD:CONV_PAD + seq, :] = a * jax.nn.sigmoid(gate)
    first = CONV_PAD - CONV_WIDTH // 2

    def tile(t, carry):
        base = pl.multiple_of(t * CONV_TILE, CONV_TILE)
        win = pad_ref[pl.ds(base, CONV_TILE + 2 * CONV_PAD), :]
        span = CONV_TILE + 2 * CONV_PAD - SUBLANES
        shifted = [win[r:r + span, :] for r in range(SUBLANES)]
        acc = jnp.zeros((CONV_TILE, BRANCH_WIDTH), F32)
        for j in range(CONV_WIDTH):
            a8, r = divmod(first + j, SUBLANES)
            acc = acc + shifted[r][a8 * SUBLANES:a8 * SUBLANES + CONV_TILE, :] * w_ref[j:j + 1, :]
        y = acc + b_ref[...]
        mu = jnp.mean(y, axis=-1, keepdims=True)
        yc = y - mu
        var = jnp.mean(yc * yc, axis=-1, keepdims=True)
        z = yc * lax.rsqrt(var + NORM_EPS) * g_ref[...] + beta_ref[...]
        o_ref[pl.ds(base, CONV_TILE), :] = (z * jax.nn.sigmoid(z)).astype(BF16)
        return carry

    lax.fori_loop(0, seq // CONV_TILE, tile, 0)


def _conv(u_cv, w, b, g, beta, batch, seq):
    return pl.pallas_call(
        functools.partial(_conv_body, seq=seq),
        grid=(batch,),
        in_specs=[pl.BlockSpec((seq, CONV_COLS), lambda i: (i, 0)),
                  _const_spec(w.shape), _const_spec(b.shape), _const_spec(g.shape), _const_spec(beta.shape)],
        out_specs=pl.BlockSpec((seq, BRANCH_WIDTH), lambda i: (i, 0)),
        out_shape=jax.ShapeDtypeStruct((batch * seq, BRANCH_WIDTH), BF16),
        scratch_shapes=[pltpu.VMEM((seq + 2 * CONV_PAD, BRANCH_WIDTH), F32)],
        compiler_params=_cparams(("parallel",), 32),
        name="conv",
    )(u_cv, w, b, g, beta)


ATT_Q_TILE = 256


def _gqa_body(q_ref, k_ref, v_ref, o_ref):
    k = k_ref[...]
    v = v_ref[...]
    lane = lax.broadcasted_iota(jnp.int32, (ATT_Q_TILE, LANES), 1)
    low = lane < HEAD_DIM
    for c in range(2):
        qc = q_ref[:, c * LANES:(c + 1) * LANES]
        halves = []
        for keep in (low, jnp.logical_not(low)):
            qh = jnp.where(keep, qc, jnp.zeros_like(qc))
            s = lax.dot_general(qh, k, (((1,), (1,)), ((), ())), preferred_element_type=F32)
            m = jnp.max(s, axis=1, keepdims=True)
            p = jnp.exp(s - m)
            l = jnp.sum(p, axis=1, keepdims=True)
            halves.append(jnp.dot(p.astype(BF16), v, preferred_element_type=F32) / l)
        o_ref[:, c * LANES:(c + 1) * LANES] = jnp.where(low, halves[0], halves[1]).astype(BF16)


def _gqa(qk, v, batch, seq):
    tiles = seq // ATT_Q_TILE
    return pl.pallas_call(
        _gqa_body,
        grid=(batch, tiles),
        in_specs=[pl.BlockSpec((ATT_Q_TILE, N_HEADS * HEAD_DIM), lambda b, i: (b * tiles + i, 0)),
                  pl.BlockSpec((seq, LANES), lambda b, i: (b, 2)),
                  pl.BlockSpec((seq, LANES), lambda b, i: (b, 0))],
        out_specs=pl.BlockSpec((ATT_Q_TILE, BRANCH_WIDTH), lambda b, i: (b * tiles + i, 0)),
        out_shape=jax.ShapeDtypeStruct((batch * seq, BRANCH_WIDTH), BF16),
        compiler_params=_cparams(("parallel", "parallel"), 48),
        name="gqa",
    )(qk, qk, v)


def _merge_body(x_ref, g_ref, b0_ref, b1_ref, b2_ref, b3_ref, wg_ref, bg_ref, wb_ref, wo_ref, o_ref):
    x = x_ref[...]
    h = _rms(x, g_ref[...]).astype(BF16)
    acc = jnp.zeros(x.shape, F32)
    for n, br_ref in enumerate((b0_ref, b1_ref, b2_ref, b3_ref)):
        cols = slice(n * D_MODEL, (n + 1) * D_MODEL)
        gl = jnp.dot(h, wg_ref[:, cols], preferred_element_type=F32) + bg_ref[:, cols]
        pr = jnp.dot(br_ref[...], wb_ref[n], preferred_element_type=F32)
        acc = acc + jax.nn.sigmoid(gl) * pr
    o_ref[...] = x + jnp.dot(acc.astype(BF16), wo_ref[...], preferred_element_type=F32)


def _merge(x, g, branches, w_gate, b_gate, w_branch, w_out):
    n_tok = x.shape[0]
    tm = TOKEN_TILE
    row = lambda i: (i, 0)
    return pl.pallas_call(
        _merge_body,
        grid=(n_tok // tm,),
        in_specs=[pl.BlockSpec((tm, D_MODEL), row), _const_spec((1, D_MODEL))]
        + [pl.BlockSpec((tm, BRANCH_WIDTH), row)] * N_BRANCHES
        + [_const_spec(w_gate.shape), _const_spec(b_gate.shape), _const_spec(w_branch.shape), _const_spec(w_out.shape)],
        out_specs=pl.BlockSpec((tm, D_MODEL), row),
        out_shape=jax.ShapeDtypeStruct((n_tok, D_MODEL), F32),
        compiler_params=_cparams(("parallel",), 56),
        name="merge",
    )(x, g, *branches, w_gate, b_gate, w_branch, w_out)


MOE_TOKEN_TILE = 1024


def _router(h, wr_ref, br_ref):
    logits = jnp.dot(h, wr_ref[...], precision=lax.Precision.HIGHEST, preferred_element_type=F32) + br_ref[...]
    lane = lax.broadcasted_iota(jnp.int32, logits.shape, 1)
    big = jnp.int32(ROUTER_LANES)
    is_g = lane < MOE_GROUPS
    gl = jnp.where(is_g, logits, NEG_INF)
    g_max = jnp.max(gl, axis=1, keepdims=True)
    g_idx = jnp.min(jnp.where(is_g & (gl == g_max), lane, big), axis=1, keepdims=True)
    g_gate = 1.0 / jnp.sum(jnp.where(is_g, jnp.exp(gl - g_max), 0.0), axis=1, keepdims=True)
    e_lo = ROUTER_E0 + g_idx * MOE_EXPERTS_PER_GROUP
    in_grp = (lane >= e_lo) & (lane < e_lo + MOE_EXPERTS_PER_GROUP)
    el = jnp.where(in_grp, logits, NEG_INF)
    top1 = jnp.max(el, axis=1, keepdims=True)
    idx1 = jnp.min(jnp.where(in_grp & (el == top1), lane, big), axis=1, keepdims=True)
    rest = in_grp & (lane != idx1)
    el2 = jnp.where(rest, logits, NEG_INF)
    top2 = jnp.max(el2, axis=1, keepdims=True)
    idx2 = jnp.min(jnp.where(rest & (el2 == top2), lane, big), axis=1, keepdims=True)
    e2 = jnp.exp(top2 - top1)
    w1 = g_gate / (1.0 + e2)
    w2 = g_gate * e2 / (1.0 + e2)
    return jnp.where(lane == idx1, w1, 0.0) + jnp.where(lane == idx2, w2, 0.0)


def _moe_body(x_ref, g_ref, wr_ref, br_ref, w1_ref, w3_ref, w2_ref, fg_ref, o_ref, h_ref, comb_ref, acc_ref, *, final):
    e = pl.program_id(1)

    @pl.when(e == 0)
    def _():
        h = _rms(x_ref[...], g_ref[...])
        h_ref[...] = h.astype(BF16)
        comb_ref[...] = _router(h, wr_ref, br_ref)
        acc_ref[...] = jnp.zeros_like(acc_ref)

    h = h_ref[...]
    a = jnp.dot(h, w1_ref[0], preferred_element_type=F32)
    b = jnp.dot(h, w3_ref[0], preferred_element_type=F32)
    comb = comb_ref[...]
    lane = lax.broadcasted_iota(jnp.int32, comb.shape, 1)
    c = jnp.sum(jnp.where(lane == ROUTER_E0 + e, comb, 0.0), axis=1, keepdims=True)
    hid = (a * jax.nn.sigmoid(a) * b * c).astype(BF16)
    acc_ref[...] += jnp.dot(hid, w2_ref[0], preferred_element_type=F32)

    @pl.when(e == MOE_N_EXPERTS - 1)
    def _():
        y = x_ref[...] + acc_ref[...]
        o_ref[...] = _rms(y, fg_ref[...]) if final else y


def _moe(x, g, wr, br, w1, w3, w2, final_g, final):
    n_tok = x.shape[0]
    tm = MOE_TOKEN_TILE
    row = lambda i, e: (i, 0)
    exp = lambda i, e: (e, 0, 0)
    return pl.pallas_call(
        functools.partial(_moe_body, final=final),
        grid=(n_tok // tm, MOE_N_EXPERTS),
        in_specs=[pl.BlockSpec((tm, D_MODEL), row), _const_spec((1, D_MODEL)),
                  _const_spec(wr.shape), _const_spec(br.shape),
                  pl.BlockSpec((1, D_MODEL, MOE_D_EXPERT), exp), pl.BlockSpec((1, D_MODEL, MOE_D_EXPERT), exp),
                  pl.BlockSpec((1, MOE_D_EXPERT, D_MODEL), exp), _const_spec((1, D_MODEL))],
        out_specs=pl.BlockSpec((tm, D_MODEL), row),
        out_shape=jax.ShapeDtypeStruct((n_tok, D_MODEL), F32),
        scratch_shapes=[pltpu.VMEM((tm, D_MODEL), BF16), pltpu.VMEM((tm, ROUTER_LANES), F32),
                        pltpu.VMEM((tm, D_MODEL), F32)],
        compiler_params=_cparams(("parallel", "arbitrary"), 48),
        name="moe",
    )(x, g, wr, br, w1, w3, w2, final_g)


def _rope_tables(seq):
    t = jnp.arange(seq)
    pos = jnp.stack([t // GRID_W, t % GRID_W], axis=-1).astype(F32)
    inv = ROPE_THETA ** (-jnp.arange(0, ROPE_AXIS_DIM, 2, dtype=F32) / ROPE_AXIS_DIM)
    ang = pos[..., None] * inv
    ang = jnp.concatenate([ang, ang], axis=-1).reshape(seq, HEAD_DIM)
    first = (jnp.arange(HEAD_DIM) % ROPE_AXIS_DIM) < (ROPE_AXIS_DIM // 2)
    cos = jnp.cos(ang)
    sin = jnp.where(first[None, :], -jnp.sin(ang), jnp.sin(ang))
    reps = QK_COLS // HEAD_DIM
    return jnp.tile(cos, (1, reps)), jnp.tile(sin, (1, reps))


ATT_HEAD_ORDER = (0, 2, 1, 3)


def _pack_layer(l, w_in, na_rpb, ml_gate_b, ml_norm_g, conv_dw_w, conv_dw_b, conv_norm_g, conv_norm_b,
                att_q_norm_g, att_k_norm_g, w_branch, w_gate, b_gate, w_out, moe_w_group, moe_b_group,
                moe_w_expert, moe_b_expert, moe_w1, moe_w3, moe_w2):
    scale = HEAD_DIM ** -0.5
    w = w_in[l]
    o0 = 0
    wna = w[:, o0:o0 + NA_COLS]
    wna = jnp.concatenate([wna[:, :BRANCH_WIDTH] * scale, wna[:, BRANCH_WIDTH:]], axis=1)
    o0 += NA_COLS
    wml = w[:, o0:o0 + 4 * BRANCH_WIDTH]
    wml = jnp.concatenate([wml[:, :BRANCH_WIDTH], wml[:, BRANCH_WIDTH:2 * BRANCH_WIDTH] * scale,
                           wml[:, 2 * BRANCH_WIDTH:]], axis=1)
    o0 += 4 * BRANCH_WIDTH
    n_gates = ML_N_GATES * N_HEADS
    wg = jnp.pad(w[:, o0:o0 + n_gates], ((0, 0), (0, LANES - n_gates)))
    o0 += n_gates
    wcv = w[:, o0:o0 + CONV_COLS]
    o0 += CONV_COLS
    wat = w[:, o0:o0 + ATT_COLS]
    wq = wat[:, :N_HEADS * HEAD_DIM].reshape(D_MODEL, N_HEADS, HEAD_DIM)[:, ATT_HEAD_ORDER, :]
    wat = jnp.concatenate([wq.reshape(D_MODEL, N_HEADS * HEAD_DIM), wat[:, N_HEADS * HEAD_DIM:]], axis=1)
    qkg = jnp.concatenate([jnp.tile(att_q_norm_g[l] * scale, N_HEADS), jnp.tile(att_k_norm_g[l], ATT_KV_HEADS)])
    wb = w_branch[l]
    wb3 = wb[3].reshape(N_HEADS, HEAD_DIM, D_MODEL)[ATT_HEAD_ORDER, :, :].reshape(BRANCH_WIDTH, D_MODEL)
    wb = jnp.concatenate([wb[:3], wb3[None]], axis=0)
    wr = jnp.concatenate([moe_w_group[l], moe_w_expert[l].reshape(D_MODEL, MOE_N_EXPERTS)], axis=1)
    n_r = MOE_GROUPS + MOE_N_EXPERTS
    br = jnp.concatenate([moe_b_group[l], moe_b_expert[l].reshape(MOE_N_EXPERTS)])
    return dict(
        wna=wna.astype(BF16), wml=wml.astype(BF16), wg=wg.astype(BF16), wcv=wcv.astype(BF16), wat=wat.astype(BF16),
        qkg=qkg.reshape(1, QK_COLS).astype(F32),
        na_bias=_na_bias_table(na_rpb[l]),
        gate_b=jnp.pad(ml_gate_b[l].reshape(1, n_gates), ((0, 0), (0, LANES - n_gates))).astype(F32),
        ml_norm_g=ml_norm_g[l].reshape(1, BRANCH_WIDTH),
        conv_w=conv_dw_w[l], conv_b=conv_dw_b[l].reshape(1, BRANCH_WIDTH),
        conv_g=conv_norm_g[l].reshape(1, BRANCH_WIDTH), conv_beta=conv_norm_b[l].reshape(1, BRANCH_WIDTH),
        w_gate=w_gate[l].astype(BF16), b_gate=b_gate[l].reshape(1, N_BRANCHES * D_MODEL),
        w_branch=wb.astype(BF16), w_out=w_out[l].astype(BF16),
        wr=jnp.pad(wr, ((0, 0), (0, ROUTER_LANES - n_r))), br=jnp.pad(br, (0, ROUTER_LANES - n_r)).reshape(1, ROUTER_LANES),
        w1=moe_w1[l].astype(BF16), w3=moe_w3[l].astype(BF16), w2=moe_w2[l].astype(BF16),
    )


def _trunk(x, layers, norm_mix_g, norm_ffn_g, final_norm_g, ones_qk, depth):
    batch, seq, _ = x.shape
    xf = x.reshape(batch * seq, D_MODEL)
    cos_t, sin_t = _rope_tables(seq)
    fg = final_norm_g.reshape(1, D_MODEL)
    for l in range(depth):
        p = layers[l]
        gm = norm_mix_g[l].reshape(1, D_MODEL)
        u_na, u_ml, gates, u_cv, qk, v = _inproj(xf, gm, p["wna"], p["wml"], p["wg"], p["wcv"], p["wat"], ones_qk,
                                                 p["qkg"], cos_t, sin_t, seq)
        o_na = _na(u_na, p["na_bias"], batch, seq)
        o_ml = _mlstm(u_ml, gates, p["gate_b"], p["ml_norm_g"], batch, seq)
        o_cv = _conv(u_cv, p["conv_w"], p["conv_b"], p["conv_g"], p["conv_beta"], batch, seq)
        o_at = _gqa(qk, v, batch, seq)
        xf = _merge(xf, gm, (o_na, o_ml, o_cv, o_at), p["w_gate"], p["b_gate"], p["w_branch"], p["w_out"])
        xf = _moe(xf, norm_ffn_g[l].reshape(1, D_MODEL), p["wr"], p["br"], p["w1"], p["w3"], p["w2"], fg,
                  final=(l == depth - 1))
    return xf.reshape(batch, seq, D_MODEL)


def kernel(x_prompt, x_sample, norm_mix_g, w_in, na_rpb, ml_gate_b, ml_norm_g, conv_dw_w, conv_dw_b, conv_norm_g, conv_norm_b, att_q_norm_g, att_k_norm_g, w_branch, w_gate, b_gate, w_out, norm_ffn_g, moe_w_group, moe_b_group, moe_w_expert, moe_b_expert, moe_w1, moe_w3, moe_w2, final_norm_g):
    depth = w_in.shape[0]
    layers = [_pack_layer(l, w_in, na_rpb, ml_gate_b, ml_norm_g, conv_dw_w, conv_dw_b, conv_norm_g, conv_norm_b,
                          att_q_norm_g, att_k_norm_g, w_branch, w_gate, b_gate, w_out, moe_w_group, moe_b_group,
                          moe_w_expert, moe_b_expert, moe_w1, moe_w3, moe_w2) for l in range(depth)]
    hid = np.arange(QK_COLS) // HEAD_DIM
    ones_qk = jnp.asarray(hid[:, None] == hid[None, :], BF16)
    y_prompt = _trunk(x_prompt, layers, norm_mix_g, norm_ffn_g, final_norm_g, ones_qk, depth)
    y_sample = _trunk(x_sample, layers, norm_mix_g, norm_ffn_g, final_norm_g, ones_qk, depth)
    return (y_prompt, y_sample)
```

```python
import functools

import jax
import jax.numpy as jnp
import numpy as np
from jax import lax
from jax.experimental import pallas as pl
from jax.experimental.pallas import tpu as pltpu

F32 = jnp.float32
BF16 = jnp.bfloat16

D_MODEL = 1024
GRID_W = 64
HEAD_DIM = 64
BRANCH_WIDTH = 256
N_BRANCHES = 4
N_HEADS = 4
NA_KH = 8
NA_KW = 16
ML_CHUNK = 64
ML_N_GATES = 4
CONV_WIDTH = 31
ATT_KV_HEADS = 2
ROPE_THETA = 10000.0
ROPE_AXIS_DIM = HEAD_DIM // 2
MOE_GROUPS = 4
MOE_EXPERTS_PER_GROUP = 4
MOE_N_EXPERTS = 16
MOE_D_EXPERT = 256
NORM_EPS = 1e-6
NEG_INF = -1e30

NA_COLS = 3 * BRANCH_WIDTH
ML_COLS = 4 * BRANCH_WIDTH + ML_N_GATES * N_HEADS
CONV_COLS = 2 * BRANCH_WIDTH
ATT_COLS = (N_HEADS + 2 * ATT_KV_HEADS) * HEAD_DIM
QK_COLS = (N_HEADS + ATT_KV_HEADS) * HEAD_DIM

LANES = 128
SUBLANES = 8
ROUTER_LANES = LANES
ROUTER_E0 = MOE_GROUPS
V7X_VMEM_BYTES = 64 * 1024 * 1024

TOKEN_TILE = 512


def _cparams(semantics, vmem_mb):
    assert vmem_mb * 1024 * 1024 < V7X_VMEM_BYTES
    return pltpu.CompilerParams(dimension_semantics=semantics, vmem_limit_bytes=vmem_mb * 1024 * 1024)


def _const_spec(shape):
    nd = len(shape)
    return pl.BlockSpec(shape, lambda *_: (0,) * nd)


def _rms(x, g):
    ms = jnp.mean(x * x, axis=-1, keepdims=True)
    return x * lax.rsqrt(ms + NORM_EPS) * g


def _inproj_body(x_ref, g_ref, wna_ref, wml_ref, wg_ref, wcv_ref, wat_ref, ones_ref, qkg_ref, cos_ref, sin_ref,
                 una_ref, uml_ref, gates_ref, ucv_ref, qk_ref, v_ref):
    h = _rms(x_ref[...], g_ref[...]).astype(BF16)
    una_ref[...] = jnp.dot(h, wna_ref[...], preferred_element_type=F32).astype(BF16)
    uml_ref[...] = jnp.dot(h, wml_ref[...], preferred_element_type=F32).astype(BF16)
    gates_ref[...] = jnp.dot(h, wg_ref[...], preferred_element_type=F32)
    ucv_ref[...] = jnp.dot(h, wcv_ref[...], preferred_element_type=F32).astype(BF16)
    ua = jnp.dot(h, wat_ref[...], preferred_element_type=F32)
    qk = ua[:, :QK_COLS]
    ms = jnp.dot((qk * qk).astype(BF16), ones_ref[...], preferred_element_type=F32) * (1.0 / HEAD_DIM)
    qn = qk * lax.rsqrt(ms + NORM_EPS) * qkg_ref[...]
    half = ROPE_AXIS_DIM // 2
    rot = []
    for c in range(QK_COLS // LANES):
        xc = qn[:, c * LANES:(c + 1) * LANES]
        lane = lax.broadcasted_iota(jnp.int32, xc.shape, 1)
        rot.append(jnp.where((lane % ROPE_AXIS_DIM) < half, pltpu.roll(xc, LANES - half, 1), pltpu.roll(xc, half, 1)))
    rot = jnp.concatenate(rot, axis=1)
    qk_ref[...] = (qn * cos_ref[...] + rot * sin_ref[...]).astype(BF16)
    v_ref[...] = ua[:, QK_COLS:].astype(BF16)


def _inproj(x, g, wna, wml, wg, wcv, wat, ones_qk, qkg, cos_t, sin_t, seq):
    n_tok = x.shape[0]
    tm = TOKEN_TILE
    tiles_per_seq = seq // tm
    row = lambda i: (i, 0)
    pos = lambda i: (i % tiles_per_seq, 0)
    outs = [
        jax.ShapeDtypeStruct((n_tok, NA_COLS), BF16),
        jax.ShapeDtypeStruct((n_tok, 4 * BRANCH_WIDTH), BF16),
        jax.ShapeDtypeStruct((n_tok, LANES), F32),
        jax.ShapeDtypeStruct((n_tok, CONV_COLS), BF16),
        jax.ShapeDtypeStruct((n_tok, QK_COLS), BF16),
        jax.ShapeDtypeStruct((n_tok, ATT_KV_HEADS * HEAD_DIM), BF16),
    ]
    return pl.pallas_call(
        _inproj_body,
        grid=(n_tok // tm,),
        in_specs=[
            pl.BlockSpec((tm, D_MODEL), row),
            _const_spec((1, D_MODEL)),
            _const_spec(wna.shape), _const_spec(wml.shape), _const_spec(wg.shape), _const_spec(wcv.shape),
            _const_spec(wat.shape), _const_spec(ones_qk.shape), _const_spec(qkg.shape),
            pl.BlockSpec((tm, QK_COLS), pos), pl.BlockSpec((tm, QK_COLS), pos),
        ],
        out_specs=[pl.BlockSpec((tm, o.shape[1]), row) for o in outs],
        out_shape=outs,
        compiler_params=_cparams(("parallel",), 48),
        name="inproj",
    )(x, g, wna, wml, wg, wcv, wat, ones_qk, qkg, cos_t, sin_t)


def _block_diag_mask(rows, cols):
    r = lax.broadcasted_iota(jnp.int32, (rows, cols), 0) // HEAD_DIM
    c = lax.broadcasted_iota(jnp.int32, (rows, cols), 1) // HEAD_DIM
    return r == (c % N_HEADS)


def _tile_heads(x):
    t = jnp.concatenate([x] * N_HEADS, axis=0)
    return jnp.where(_block_diag_mask(N_HEADS * HEAD_DIM, BRANCH_WIDTH), t, jnp.zeros_like(t))


def _na_body(q_ref, k_ref, v_ref, bias_ref, o_ref, *, rows):
    n_keys = NA_KH * GRID_W

    def one_row(r, carry):
        r0 = jnp.clip(r - NA_KH // 2, 0, rows - NA_KH)
        delta = r0 - r + NA_KH - 1
        q = q_ref[pl.ds(pl.multiple_of(r * GRID_W, GRID_W), GRID_W), :]
        kk = k_ref[pl.ds(pl.multiple_of(r0 * GRID_W, GRID_W), n_keys), :]
        vv = v_ref[pl.ds(pl.multiple_of(r0 * GRID_W, GRID_W), n_keys), :]
        s = lax.dot_general(_tile_heads(q), kk, (((1,), (1,)), ((), ())), preferred_element_type=F32)
        s = s + bias_ref[delta]
        m = jnp.max(s, axis=1, keepdims=True)
        p = jnp.exp(s - m)
        l = jnp.sum(p, axis=1, keepdims=True)
        of = jnp.dot(p.astype(BF16), vv, preferred_element_type=F32) / l
        of = jnp.where(_block_diag_mask(N_HEADS * HEAD_DIM, BRANCH_WIDTH), of, 0.0)
        o = of[0:HEAD_DIM] + of[HEAD_DIM:2 * HEAD_DIM] + of[2 * HEAD_DIM:3 * HEAD_DIM] + of[3 * HEAD_DIM:]
        o_ref[pl.ds(pl.multiple_of(r * GRID_W, GRID_W), GRID_W), :] = o.astype(BF16)
        return carry

    lax.fori_loop(0, rows, one_row, 0)


def _na_bias_table(rpb):
    qc = np.arange(GRID_W)[:, None]
    kc = np.arange(GRID_W)[None, :]
    win = np.clip(qc - NA_KW // 2, 0, GRID_W - NA_KW)
    in_win = (kc >= win) & (kc < win + NA_KW)
    col_off = np.clip(kc - qc + NA_KW - 1, 0, 2 * NA_KW - 2)
    n_ro, n_co = 2 * NA_KH - 1, 2 * NA_KW - 1
    pick = np.zeros((n_co, GRID_W * GRID_W), np.float32)
    pick[col_off.reshape(-1), np.arange(GRID_W * GRID_W)] = 1.0
    c = jnp.dot(rpb.astype(F32).reshape(N_HEADS * n_ro, n_co), pick, precision=lax.Precision.HIGHEST)
    c = jnp.where(in_win.reshape(1, -1), c, NEG_INF).reshape(N_HEADS, n_ro, GRID_W, GRID_W)
    c = c.transpose(0, 2, 1, 3)
    b = jnp.stack([c[:, :, d:d + NA_KH, :] for d in range(NA_KH)], axis=0)
    return b.reshape(NA_KH, N_HEADS * GRID_W, NA_KH * GRID_W)


def _na(u_na, bias, batch, seq):
    rows = seq // GRID_W
    return pl.pallas_call(
        functools.partial(_na_body, rows=rows),
        grid=(batch,),
        in_specs=[
            pl.BlockSpec((seq, BRANCH_WIDTH), lambda b: (b, 0)),
            pl.BlockSpec((seq, BRANCH_WIDTH), lambda b: (b, 1)),
            pl.BlockSpec((seq, BRANCH_WIDTH), lambda b: (b, 2)),
            _const_spec(bias.shape),
        ],
        out_specs=pl.BlockSpec((seq, BRANCH_WIDTH), lambda b: (b, 0)),
        out_shape=jax.ShapeDtypeStruct((batch * seq, BRANCH_WIDTH), BF16),
        compiler_params=_cparams(("parallel",), 48),
        name="na",
    )(u_na, u_na, u_na, bias)


def _expand_heads(cols):
    hid = lax.broadcasted_iota(jnp.int32, (ML_CHUNK, BRANCH_WIDTH), 1) // HEAD_DIM
    out = jnp.broadcast_to(cols[:, 0:1], (ML_CHUNK, BRANCH_WIDTH))
    for h in range(1, N_HEADS):
        out = jnp.where(hid == h, cols[:, h:h + 1], out)
    return out


def _head_rowmax(x):
    hid = lax.broadcasted_iota(jnp.int32, x.shape, 1) // HEAD_DIM
    out = jnp.zeros_like(x)
    for h in range(N_HEADS):
        mh = jnp.max(jnp.where(hid == h, x, NEG_INF), axis=1, keepdims=True)
        out = jnp.where(hid == h, mh, out)
    return out


def _mlstm_step_fn(reverse, q_ref, k_ref, v_ref, gates_ref, gb_ref, h_ref, state_ref, n_chunks):
    L = ML_CHUNK
    W = BRANCH_WIDTH
    sub = lax.broadcasted_iota(jnp.int32, (L, W), 0)
    pos = lax.broadcasted_iota(jnp.int32, (L, W), 1) % HEAD_DIM
    diag = sub == pos
    causal = (pos >= sub) if reverse else (pos <= sub)
    tr = lax.broadcasted_iota(jnp.int32, (L, L), 0)
    tc = lax.broadcasted_iota(jnp.int32, (L, L), 1)
    tri = ((tc >= tr) if reverse else (tc <= tr)).astype(F32)
    ones_bd = _block_diag_mask(W, W).astype(BF16)
    bd2 = _block_diag_mask(W, 2 * W)
    i_off, f_off = (2 * N_HEADS, 3 * N_HEADS) if reverse else (0, N_HEADS)
    state_ref[...] = jnp.zeros_like(state_ref)

    def step(ci, m_run):
        c = (n_chunks - 1 - ci) if reverse else ci
        sl = pl.ds(pl.multiple_of(c * L, L), L)
        q = q_ref[sl, :]
        k = k_ref[sl, :]
        v = v_ref[sl, :]
        g = gates_ref[sl, :] + gb_ref[...]
        i_e = _expand_heads(g[:, i_off:i_off + N_HEADS])
        f_e = _expand_heads(jax.nn.log_sigmoid(g[:, f_off:f_off + N_HEADS]))
        b = jnp.dot(tri, f_e, precision=lax.Precision.HIGHEST, preferred_element_type=F32)
        b_end = b[0:1, :] if reverse else b[L - 1:L, :]
        rrow = jnp.sum(jnp.where(diag, i_e - b, 0.0), axis=0, keepdims=True)
        log_d = jnp.where(causal, b + rrow, NEG_INF)
        inter = b + m_run
        m_row = jnp.maximum(inter, _head_rowmax(log_d))
        d = jnp.exp(log_d - m_row)
        w_inter = jnp.exp(inter - m_row)
        s = lax.dot_general(q, _tile_heads(k), (((1,), (1,)), ((), ())), preferred_element_type=F32)
        p = (s * d).astype(BF16)
        r_loc = jnp.dot(p, jnp.concatenate([_tile_heads(v), ones_bd], axis=1), preferred_element_type=F32)
        r_int = jnp.dot(q, state_ref[...].astype(BF16), preferred_element_type=F32)
        num = r_loc[:, :W] + w_inter * r_int[:, :W]
        den = r_loc[:, W:] + w_inter * r_int[:, W:]
        h_ref[sl, :] = num / jnp.maximum(jnp.abs(den), jnp.exp(-m_row))
        log_w = b_end - b + i_e
        m_new = jnp.maximum(b_end + m_run, jnp.max(log_w, axis=0, keepdims=True))
        w = jnp.exp(log_w - m_new)
        decay = jnp.exp(b_end + m_run - m_new)
        upd_rhs = jnp.concatenate([(w * v.astype(F32)).astype(BF16), w.astype(BF16)], axis=1)
        upd = lax.dot_general(k, upd_rhs, (((0,), (0,)), ((), ())), preferred_element_type=F32)
        upd = jnp.where(bd2, upd, 0.0)
        state_ref[...] = jnp.concatenate([decay, decay], axis=1) * state_ref[...] + upd
        return m_new

    return step


def _mlstm_body(q_ref, k_ref, v_ref, og_ref, gates_ref, gb_ref, ng_ref, o_ref, hf_ref, hb_ref, sf_ref, sb_ref, *, seq):
    n_chunks = seq // ML_CHUNK
    step_f = _mlstm_step_fn(False, q_ref, k_ref, v_ref, gates_ref, gb_ref, hf_ref, sf_ref, n_chunks)
    step_b = _mlstm_step_fn(True, q_ref, k_ref, v_ref, gates_ref, gb_ref, hb_ref, sb_ref, n_chunks)
    m0 = jnp.full((1, BRANCH_WIDTH), NEG_INF, F32)
    lax.fori_loop(0, n_chunks, lambda ci, m: (step_f(ci, m[0]), step_b(ci, m[1])), (m0, m0))
    ones_bd = _block_diag_mask(BRANCH_WIDTH, BRANCH_WIDTH).astype(BF16)
    tile = 256

    def finish(t, carry):
        sl = pl.ds(pl.multiple_of(t * tile, tile), tile)
        hs = hf_ref[sl, :] + hb_ref[sl, :]
        ms = jnp.dot((hs * hs).astype(BF16), ones_bd, preferred_element_type=F32) * (1.0 / HEAD_DIM)
        hm = hs * lax.rsqrt(ms + NORM_EPS) * ng_ref[...]
        o_ref[sl, :] = (hm * jax.nn.sigmoid(og_ref[sl, :].astype(F32))).astype(BF16)
        return carry

    lax.fori_loop(0, seq // tile, finish, 0)


def _mlstm(u_ml, gates, gate_b, norm_g, batch, seq):
    col = lambda j: pl.BlockSpec((seq, BRANCH_WIDTH), lambda b: (b, j))
    return pl.pallas_call(
        functools.partial(_mlstm_body, seq=seq),
        grid=(batch,),
        in_specs=[col(0), col(1), col(2), col(3),
                  pl.BlockSpec((seq, LANES), lambda b: (b, 0)),
                  _const_spec((1, LANES)), _const_spec((1, BRANCH_WIDTH))],
        out_specs=pl.BlockSpec((seq, BRANCH_WIDTH), lambda b: (b, 0)),
        out_shape=jax.ShapeDtypeStruct((batch * seq, BRANCH_WIDTH), BF16),
        scratch_shapes=[pltpu.VMEM((seq, BRANCH_WIDTH), F32), pltpu.VMEM((seq, BRANCH_WIDTH), F32),
                        pltpu.VMEM((BRANCH_WIDTH, 2 * BRANCH_WIDTH), F32),
                        pltpu.VMEM((BRANCH_WIDTH, 2 * BRANCH_WIDTH), F32)],
        compiler_params=_cparams(("parallel",), 48),
        name="mlstm",
    )(u_ml, u_ml, u_ml, u_ml, gates, gate_b, norm_g)


CONV_PAD = 16
CONV_TILE = 64


def _conv_body(u_ref, w_ref, b_ref, g_ref, beta_ref, o_ref, pad_ref, *, seq):
    u = u_ref[...]
    a = u[:, :BRANCH_WIDTH].astype(F32)
    gate = u[:, BRANCH_WIDTH:].astype(F32)
    zeros = jnp.zeros((CONV_PAD, BRANCH_WIDTH), F32)
    pad_ref[0:CONV_PAD, :] = zeros
    pad_ref[CONV_PAD + seq:, :] = zeros
    pad_ref[CONV_PAD:CONV_PAD + seq, :] = a * jax.nn.sigmoid(gate)
    first = CONV_PAD - CONV_WIDTH // 2

    def tile(t, carry):
        base = pl.multiple_of(t * CONV_TILE, CONV_TILE)
        win = pad_ref[pl.ds(base, CONV_TILE + 2 * CONV_PAD), :]
        span = CONV_TILE + 2 * CONV_PAD - SUBLANES
        shifted = [win[r:r + span, :] for r in range(SUBLANES)]
        acc = jnp.zeros((CONV_TILE, BRANCH_WIDTH), F32)
        for j in range(CONV_WIDTH):
            a8, r = divmod(first + j, SUBLANES)
            acc = acc + shifted[r][a8 * SUBLANES:a8 * SUBLANES + CONV_TILE, :] * w_ref[j:j + 1, :]
        y = acc + b_ref[...]
        mu = jnp.mean(y, axis=-1, keepdims=True)
        yc = y - mu
        var = jnp.mean(yc * yc, axis=-1, keepdims=True)
        z = yc * lax.rsqrt(var + NORM_EPS) * g_ref[...] + beta_ref[...]
        o_ref[pl.ds(base, CONV_TILE), :] = (z * jax.nn.sigmoid(z)).astype(BF16)
        return carry

    lax.fori_loop(0, seq // CONV_TILE, tile, 0)


def _conv(u_cv, w, b, g, beta, batch, seq):
    return pl.pallas_call(
        functools.partial(_conv_body, seq=seq),
        grid=(batch,),
        in_specs=[pl.BlockSpec((seq, CONV_COLS), lambda i: (i, 0)),
                  _const_spec(w.shape), _const_spec(b.shape), _const_spec(g.shape), _const_spec(beta.shape)],
        out_specs=pl.BlockSpec((seq, BRANCH_WIDTH), lambda i: (i, 0)),
        out_shape=jax.ShapeDtypeStruct((batch * seq, BRANCH_WIDTH), BF16),
        scratch_shapes=[pltpu.VMEM((seq + 2 * CONV_PAD, BRANCH_WIDTH), F32)],
        compiler_params=_cparams(("parallel",), 32),
        name="conv",
    )(u_cv, w, b, g, beta)


ATT_Q_TILE = 256


def _gqa_body(q_ref, k_ref, v_ref, o_ref):
    k = k_ref[...]
    v = v_ref[...]
    lane = lax.broadcasted_iota(jnp.int32, (ATT_Q_TILE, LANES), 1)
    low = lane < HEAD_DIM
    for c in range(2):
        qc = q_ref[:, c * LANES:(c + 1) * LANES]
        halves = []
        for keep in (low, jnp.logical_not(low)):
            qh = jnp.where(keep, qc, jnp.zeros_like(qc))
            s = lax.dot_general(qh, k, (((1,), (1,)), ((), ())), preferred_element_type=F32)
            m = jnp.max(s, axis=1, keepdims=True)
            p = jnp.exp(s - m)
            l = jnp.sum(p, axis=1, keepdims=True)
            halves.append(jnp.dot(p.astype(BF16), v, preferred_element_type=F32) / l)
        o_ref[:, c * LANES:(c + 1) * LANES] = jnp.where(low, halves[0], halves[1]).astype(BF16)


def _gqa(qk, v, batch, seq):
    tiles = seq // ATT_Q_TILE
    return pl.pallas_call(
        _gqa_body,
        grid=(batch, tiles),
        in_specs=[pl.BlockSpec((ATT_Q_TILE, N_HEADS * HEAD_DIM), lambda b, i: (b * tiles + i, 0)),
                  pl.BlockSpec((seq, LANES), lambda b, i: (b, 2)),
                  pl.BlockSpec((seq, LANES), lambda b, i: (b, 0))],
        out_specs=pl.BlockSpec((ATT_Q_TILE, BRANCH_WIDTH), lambda b, i: (b * tiles + i, 0)),
        out_shape=jax.ShapeDtypeStruct((batch * seq, BRANCH_WIDTH), BF16),
        compiler_params=_cparams(("parallel", "parallel"), 48),
        name="gqa",
    )(qk, qk, v)


def _merge_body(x_ref, g_ref, b0_ref, b1_ref, b2_ref, b3_ref, wg_ref, bg_ref, wb_ref, wo_ref, g2_ref, wr_ref, br_ref,
                o_ref):
    x = x_ref[...]
    h = _rms(x, g_ref[...]).astype(BF16)
    acc = jnp.zeros(x.shape, F32)
    for n, br_n_ref in enumerate((b0_ref, b1_ref, b2_ref, b3_ref)):
        cols = slice(n * D_MODEL, (n + 1) * D_MODEL)
        gl = jnp.dot(h, wg_ref[:, cols], preferred_element_type=F32) + bg_ref[:, cols]
        pr = jnp.dot(br_n_ref[...], wb_ref[n], preferred_element_type=F32)
        acc = acc + jax.nn.sigmoid(gl) * pr
    y = x + jnp.dot(acc.astype(BF16), wo_ref[...], preferred_element_type=F32)
    o_ref[:, :D_MODEL] = y
    o_ref[:, D_MODEL:] = _router(_rms(y, g2_ref[...]), wr_ref, br_ref)


def _merge(x, g, branches, w_gate, b_gate, w_branch, w_out, g2, wr, br):
    n_tok = x.shape[0]
    tm = TOKEN_TILE
    row = lambda i: (i, 0)
    return pl.pallas_call(
        _merge_body,
        grid=(n_tok // tm,),
        in_specs=[pl.BlockSpec((tm, D_MODEL), row), _const_spec((1, D_MODEL))]
        + [pl.BlockSpec((tm, BRANCH_WIDTH), row)] * N_BRANCHES
        + [_const_spec(w_gate.shape), _const_spec(b_gate.shape), _const_spec(w_branch.shape), _const_spec(w_out.shape),
           _const_spec((1, D_MODEL)), _const_spec(wr.shape), _const_spec(br.shape)],
        out_specs=pl.BlockSpec((tm, MOE_ROW), row),
        out_shape=jax.ShapeDtypeStruct((n_tok, MOE_ROW), F32),
        compiler_params=_cparams(("parallel",), 56),
        name="merge",
    )(x, g, *branches, w_gate, b_gate, w_branch, w_out, g2, wr, br)


MOE_TILE = 512
MOE_ROW = D_MODEL + ROUTER_LANES
ROUTER_GROUP_LANE = ROUTER_E0 + MOE_N_EXPERTS
GATHER_CHUNK = 1024


def _router(h, wr_ref, br_ref):
    logits = jnp.dot(h, wr_ref[...], precision=lax.Precision.HIGHEST, preferred_element_type=F32) + br_ref[...]
    lane = lax.broadcasted_iota(jnp.int32, logits.shape, 1)
    big = jnp.int32(ROUTER_LANES)
    is_g = lane < MOE_GROUPS
    gl = jnp.where(is_g, logits, NEG_INF)
    g_max = jnp.max(gl, axis=1, keepdims=True)
    g_idx = jnp.min(jnp.where(is_g & (gl == g_max), lane, big), axis=1, keepdims=True)
    g_gate = 1.0 / jnp.sum(jnp.where(is_g, jnp.exp(gl - g_max), 0.0), axis=1, keepdims=True)
    e_lo = ROUTER_E0 + g_idx * MOE_EXPERTS_PER_GROUP
    in_grp = (lane >= e_lo) & (lane < e_lo + MOE_EXPERTS_PER_GROUP)
    el = jnp.where(in_grp, logits, NEG_INF)
    top1 = jnp.max(el, axis=1, keepdims=True)
    idx1 = jnp.min(jnp.where(in_grp & (el == top1), lane, big), axis=1, keepdims=True)
    rest = in_grp & (lane != idx1)
    el2 = jnp.where(rest, logits, NEG_INF)
    top2 = jnp.max(el2, axis=1, keepdims=True)
    idx2 = jnp.min(jnp.where(rest & (el2 == top2), lane, big), axis=1, keepdims=True)
    e2 = jnp.exp(top2 - top1)
    w1 = g_gate / (1.0 + e2)
    w2 = g_gate * e2 / (1.0 + e2)
    rec = jnp.where(lane == idx1, w1, 0.0) + jnp.where(lane == idx2, w2, 0.0)
    return jnp.where(lane == ROUTER_GROUP_LANE, g_idx.astype(F32), rec)


def _gather_rows_body(idx_ref, table_ref, out_ref, sem):
    i = pl.program_id(0)
    base = i * GATHER_CHUNK

    def issue(j, carry):
        pltpu.make_async_copy(table_ref.at[pl.ds(idx_ref[j], 1)], out_ref.at[pl.ds(base + j, 1)], sem).start()
        return carry

    lax.fori_loop(0, GATHER_CHUNK, issue, 0, unroll=8)

    def drain_one_step():
        pltpu.make_async_copy(out_ref.at[pl.ds(0, GATHER_CHUNK)], out_ref.at[pl.ds(0, GATHER_CHUNK)], sem).wait()

    @pl.when(i > 0)
    def _():
        drain_one_step()

    @pl.when(i == pl.num_programs(0) - 1)
    def _():
        drain_one_step()


def _gather_rows(table, idx):
    n_out = idx.shape[0]
    return pl.pallas_call(
        _gather_rows_body,
        grid=(n_out // GATHER_CHUNK,),
        in_specs=[pl.BlockSpec((GATHER_CHUNK,), lambda i: (i,), memory_space=pltpu.SMEM),
                  pl.BlockSpec(memory_space=pl.ANY)],
        out_specs=pl.BlockSpec(memory_space=pl.ANY),
        out_shape=jax.ShapeDtypeStruct((n_out, table.shape[1]), table.dtype),
        scratch_shapes=[pltpu.SemaphoreType.DMA(())],
        compiler_params=_cparams(("arbitrary",), 16),
        name="gather_rows",
    )(idx, table)


def _moe_experts_body(tg_ref, tv_ref, x_ref, g_ref, w1_ref, w3_ref, w2_ref, o_ref):
    i = pl.program_id(0)

    @pl.when(tv_ref[i] > 0)
    def _():
        h = _rms(x_ref[:, :D_MODEL], g_ref[...]).astype(BF16)
        rec = x_ref[:, D_MODEL:]
        lane = lax.broadcasted_iota(jnp.int32, rec.shape, 1)
        e0 = ROUTER_E0 + tg_ref[i] * MOE_EXPERTS_PER_GROUP
        acc = jnp.zeros((MOE_TILE, D_MODEL), F32)
        for e in range(MOE_EXPERTS_PER_GROUP):
            a = jnp.dot(h, w1_ref[e], preferred_element_type=F32)
            b = jnp.dot(h, w3_ref[e], preferred_element_type=F32)
            c = jnp.sum(jnp.where(lane == e0 + e, rec, 0.0), axis=1, keepdims=True)
            hid = (a * jax.nn.sigmoid(a) * b * c).astype(BF16)
            acc = acc + jnp.dot(hid, w2_ref[e], preferred_element_type=F32)
        o_ref[...] = acc

    @pl.when(tv_ref[i] == 0)
    def _():
        o_ref[...] = jnp.zeros_like(o_ref)


def _moe_experts(xs, tile_group, tile_used, g, w1, w3, w2):
    n_slots = xs.shape[0]
    grp = lambda i, tg, tv: (tg[i], 0, 0)
    grid_spec = pltpu.PrefetchScalarGridSpec(
        num_scalar_prefetch=2,
        grid=(n_slots // MOE_TILE,),
        in_specs=[pl.BlockSpec((MOE_TILE, MOE_ROW), lambda i, tg, tv: (i, 0)),
                  pl.BlockSpec((1, D_MODEL), lambda i, tg, tv: (0, 0)),
                  pl.BlockSpec((MOE_EXPERTS_PER_GROUP, D_MODEL, MOE_D_EXPERT), grp),
                  pl.BlockSpec((MOE_EXPERTS_PER_GROUP, D_MODEL, MOE_D_EXPERT), grp),
                  pl.BlockSpec((MOE_EXPERTS_PER_GROUP, MOE_D_EXPERT, D_MODEL), grp)],
        out_specs=pl.BlockSpec((MOE_TILE, D_MODEL), lambda i, tg, tv: (i, 0)),
    )
    return pl.pallas_call(
        _moe_experts_body,
        grid_spec=grid_spec,
        out_shape=jax.ShapeDtypeStruct((n_slots, D_MODEL), F32),
        compiler_params=_cparams(("arbitrary",), 48),
        name="moe_experts",
    )(tile_group, tile_used, xs, g, w1, w3, w2)


def _moe_add_body(x_ref, y_ref, fg_ref, o_ref, *, final):
    y = x_ref[...] + y_ref[...]
    o_ref[...] = _rms(y, fg_ref[...]) if final else y


def _moe_add(xr, yg, final_g, final):
    n_tok = yg.shape[0]
    tm = TOKEN_TILE
    row = lambda i: (i, 0)
    return pl.pallas_call(
        functools.partial(_moe_add_body, final=final),
        grid=(n_tok // tm,),
        in_specs=[pl.BlockSpec((tm, D_MODEL), row), pl.BlockSpec((tm, D_MODEL), row), _const_spec((1, D_MODEL))],
        out_specs=pl.BlockSpec((tm, D_MODEL), row),
        out_shape=jax.ShapeDtypeStruct((n_tok, D_MODEL), F32),
        compiler_params=_cparams(("parallel",), 32),
        name="moe_add",
    )(xr, yg, final_g)


def _moe(xr, g, w1, w3, w2, final_g, final):
    n_tok = xr.shape[0]
    n_slots = n_tok + MOE_GROUPS * MOE_TILE
    group = xr[:, D_MODEL + ROUTER_GROUP_LANE].astype(jnp.int32)
    onehot = (group[:, None] == jnp.arange(MOE_GROUPS)[None, :]).astype(jnp.int32)
    csum = jnp.cumsum(onehot, axis=0)
    counts = csum[-1]
    rank = jnp.sum(csum * onehot, axis=1) - 1
    padded = (counts + MOE_TILE - 1) // MOE_TILE * MOE_TILE
    seg_end = jnp.cumsum(padded)
    seg_start = seg_end - padded
    dest = jnp.sum(seg_start[None, :] * onehot, axis=1) + rank
    src = jnp.zeros((n_slots,), jnp.int32).at[dest].set(jnp.arange(n_tok, dtype=jnp.int32))
    tile_start = jnp.arange(n_slots // MOE_TILE, dtype=jnp.int32) * MOE_TILE
    tile_group = jnp.minimum(jnp.sum((tile_start[:, None] >= seg_end[None, :]).astype(jnp.int32), axis=1),
                             MOE_GROUPS - 1)
    tile_used = (tile_start < seg_end[-1]).astype(jnp.int32)
    xs = _gather_rows(xr, src)
    ys = _moe_experts(xs, tile_group, tile_used, g, w1, w3, w2)
    yg = _gather_rows(ys, dest)
    return _moe_add(xr, yg, final_g, final)


def _rope_tables(seq):
    t = jnp.arange(seq)
    pos = jnp.stack([t // GRID_W, t % GRID_W], axis=-1).astype(F32)
    inv = ROPE_THETA ** (-jnp.arange(0, ROPE_AXIS_DIM, 2, dtype=F32) / ROPE_AXIS_DIM)
    ang = pos[..., None] * inv
    ang = jnp.concatenate([ang, ang], axis=-1).reshape(seq, HEAD_DIM)
    first = (jnp.arange(HEAD_DIM) % ROPE_AXIS_DIM) < (ROPE_AXIS_DIM // 2)
    cos = jnp.cos(ang)
    sin = jnp.where(first[None, :], -jnp.sin(ang), jnp.sin(ang))
    reps = QK_COLS // HEAD_DIM
    return jnp.tile(cos, (1, reps)), jnp.tile(sin, (1, reps))


ATT_HEAD_ORDER = (0, 2, 1, 3)


def _pack_layer(l, w_in, na_rpb, ml_gate_b, ml_norm_g, conv_dw_w, conv_dw_b, conv_norm_g, conv_norm_b,
                att_q_norm_g, att_k_norm_g, w_branch, w_gate, b_gate, w_out, moe_w_group, moe_b_group,
                moe_w_expert, moe_b_expert, moe_w1, moe_w3, moe_w2):
    scale = HEAD_DIM ** -0.5
    w = w_in[l]
    o0 = 0
    wna = w[:, o0:o0 + NA_COLS]
    wna = jnp.concatenate([wna[:, :BRANCH_WIDTH] * scale, wna[:, BRANCH_WIDTH:]], axis=1)
    o0 += NA_COLS
    wml = w[:, o0:o0 + 4 * BRANCH_WIDTH]
    wml = jnp.concatenate([wml[:, :BRANCH_WIDTH], wml[:, BRANCH_WIDTH:2 * BRANCH_WIDTH] * scale,
                           wml[:, 2 * BRANCH_WIDTH:]], axis=1)
    o0 += 4 * BRANCH_WIDTH
    n_gates = ML_N_GATES * N_HEADS
    wg = jnp.pad(w[:, o0:o0 + n_gates], ((0, 0), (0, LANES - n_gates)))
    o0 += n_gates
    wcv = w[:, o0:o0 + CONV_COLS]
    o0 += CONV_COLS
    wat = w[:, o0:o0 + ATT_COLS]
    wq = wat[:, :N_HEADS * HEAD_DIM].reshape(D_MODEL, N_HEADS, HEAD_DIM)[:, ATT_HEAD_ORDER, :]
    wat = jnp.concatenate([wq.reshape(D_MODEL, N_HEADS * HEAD_DIM), wat[:, N_HEADS * HEAD_DIM:]], axis=1)
    qkg = jnp.concatenate([jnp.tile(att_q_norm_g[l] * scale, N_HEADS), jnp.tile(att_k_norm_g[l], ATT_KV_HEADS)])
    wb = w_branch[l]
    wb3 = wb[3].reshape(N_HEADS, HEAD_DIM, D_MODEL)[ATT_HEAD_ORDER, :, :].reshape(BRANCH_WIDTH, D_MODEL)
    wb = jnp.concatenate([wb[:3], wb3[None]], axis=0)
    wr = jnp.concatenate([moe_w_group[l], moe_w_expert[l].reshape(D_MODEL, MOE_N_EXPERTS)], axis=1)
    n_r = MOE_GROUPS + MOE_N_EXPERTS
    br = jnp.concatenate([moe_b_group[l], moe_b_expert[l].reshape(MOE_N_EXPERTS)])
    return dict(
        wna=wna.astype(BF16), wml=wml.astype(BF16), wg=wg.astype(BF16), wcv=wcv.astype(BF16), wat=wat.astype(BF16),
        qkg=qkg.reshape(1, QK_COLS).astype(F32),
        na_bias=_na_bias_table(na_rpb[l]),
        gate_b=jnp.pad(ml_gate_b[l].reshape(1, n_gates), ((0, 0), (0, LANES - n_gates))).astype(F32),
        ml_norm_g=ml_norm_g[l].reshape(1, BRANCH_WIDTH),
        conv_w=conv_dw_w[l], conv_b=conv_dw_b[l].reshape(1, BRANCH_WIDTH),
        conv_g=conv_norm_g[l].reshape(1, BRANCH_WIDTH), conv_beta=conv_norm_b[l].reshape(1, BRANCH_WIDTH),
        w_gate=w_gate[l].astype(BF16), b_gate=b_gate[l].reshape(1, N_BRANCHES * D_MODEL),
        w_branch=wb.astype(BF16), w_out=w_out[l].astype(BF16),
        wr=jnp.pad(wr, ((0, 0), (0, ROUTER_LANES - n_r))), br=jnp.pad(br, (0, ROUTER_LANES - n_r)).reshape(1, ROUTER_LANES),
        w1=moe_w1[l].astype(BF16), w3=moe_w3[l].astype(BF16), w2=moe_w2[l].astype(BF16),
    )


def _trunk(x, layers, norm_mix_g, norm_ffn_g, final_norm_g, ones_qk, depth):
    batch, seq, _ = x.shape
    xf = x.reshape(batch * seq, D_MODEL)
    cos_t, sin_t = _rope_tables(seq)
    fg = final_norm_g.reshape(1, D_MODEL)
    for l in range(depth):
        p = layers[l]
        gm = norm_mix_g[l].reshape(1, D_MODEL)
        u_na, u_ml, gates, u_cv, qk, v = _inproj(xf, gm, p["wna"], p["wml"], p["wg"], p["wcv"], p["wat"], ones_qk,
                                                 p["qkg"], cos_t, sin_t, seq)
        o_na = _na(u_na, p["na_bias"], batch, seq)
        o_ml = _mlstm(u_ml, gates, p["gate_b"], p["ml_norm_g"], batch, seq)
        o_cv = _conv(u_cv, p["conv_w"], p["conv_b"], p["conv_g"], p["conv_beta"], batch, seq)
        o_at = _gqa(qk, v, batch, seq)
        gf = norm_ffn_g[l].reshape(1, D_MODEL)
        xr = _merge(xf, gm, (o_na, o_ml, o_cv, o_at), p["w_gate"], p["b_gate"], p["w_branch"], p["w_out"],
                    gf, p["wr"], p["br"])
        xf = _moe(xr, gf, p["w1"], p["w3"], p["w2"], fg, final=(l == depth - 1))
    return xf.reshape(batch, seq, D_MODEL)


def kernel(x_prompt, x_sample, norm_mix_g, w_in, na_rpb, ml_gate_b, ml_norm_g, conv_dw_w, conv_dw_b, conv_norm_g, conv_norm_b, att_q_norm_g, att_k_norm_g, w_branch, w_gate, b_gate, w_out, norm_ffn_g, moe_w_group, moe_b_group, moe_w_expert, moe_b_expert, moe_w1, moe_w3, moe_w2, final_norm_g):
    depth = w_in.shape[0]
    layers = [_pack_layer(l, w_in, na_rpb, ml_gate_b, ml_norm_g, conv_dw_w, conv_dw_b, conv_norm_g, conv_norm_b,
                          att_q_norm_g, att_k_norm_g, w_branch, w_gate, b_gate, w_out, moe_w_group, moe_b_group,
                          moe_w_expert, moe_b_expert, moe_w1, moe_w3, moe_w2) for l in range(depth)]
    hid = np.arange(QK_COLS) // HEAD_DIM
    ones_qk = jnp.asarray(hid[:, None] == hid[None, :], BF16)
    y_prompt = _trunk(x_prompt, layers, norm_mix_g, norm_ffn_g, final_norm_g, ones_qk, depth)
    y_sample = _trunk(x_sample, layers, norm_mix_g, norm_ffn_g, final_norm_g, ones_qk, depth)
    return (y_prompt, y_sample)
```

```python
import functools

import jax
import jax.numpy as jnp
import numpy as np
from jax import lax
from jax.experimental import pallas as pl
from jax.experimental.pallas import tpu as pltpu

F32 = jnp.float32
BF16 = jnp.bfloat16

D_MODEL = 1024
GRID_W = 64
HEAD_DIM = 64
BRANCH_WIDTH = 256
N_BRANCHES = 4
N_HEADS = 4
NA_KH = 8
NA_KW = 16
ML_CHUNK = 64
ML_N_GATES = 4
CONV_WIDTH = 31
ATT_KV_HEADS = 2
ROPE_THETA = 10000.0
ROPE_AXIS_DIM = HEAD_DIM // 2
MOE_GROUPS = 4
MOE_EXPERTS_PER_GROUP = 4
MOE_N_EXPERTS = 16
MOE_D_EXPERT = 256
NORM_EPS = 1e-6
NEG_INF = -1e30

NA_COLS = 3 * BRANCH_WIDTH
ML_COLS = 4 * BRANCH_WIDTH + ML_N_GATES * N_HEADS
CONV_COLS = 2 * BRANCH_WIDTH
ATT_COLS = (N_HEADS + 2 * ATT_KV_HEADS) * HEAD_DIM
QK_COLS = (N_HEADS + ATT_KV_HEADS) * HEAD_DIM

LANES = 128
SUBLANES = 8
ROUTER_LANES = LANES
ROUTER_E0 = MOE_GROUPS
V7X_VMEM_BYTES = 64 * 1024 * 1024

TOKEN_TILE = 512


def _cparams(semantics, vmem_mb):
    assert vmem_mb * 1024 * 1024 < V7X_VMEM_BYTES
    return pltpu.CompilerParams(dimension_semantics=semantics, vmem_limit_bytes=vmem_mb * 1024 * 1024)


def _const_spec(shape):
    nd = len(shape)
    return pl.BlockSpec(shape, lambda *_: (0,) * nd)


def _rms(x, g):
    ms = jnp.mean(x * x, axis=-1, keepdims=True)
    return x * lax.rsqrt(ms + NORM_EPS) * g


def _inproj_body(x_ref, g_ref, wna_ref, wml_ref, wg_ref, wcv_ref, wat_ref, ones_ref, qkg_ref, cos_ref, sin_ref,
                 una_ref, uml_ref, gates_ref, ucv_ref, qk_ref, v_ref):
    h = _rms(x_ref[...], g_ref[...]).astype(BF16)
    una_ref[...] = jnp.dot(h, wna_ref[...], preferred_element_type=F32).astype(BF16)
    uml_ref[...] = jnp.dot(h, wml_ref[...], preferred_element_type=F32).astype(BF16)
    gates_ref[...] = jnp.dot(h, wg_ref[...], preferred_element_type=F32)
    ucv_ref[...] = jnp.dot(h, wcv_ref[...], preferred_element_type=F32).astype(BF16)
    ua = jnp.dot(h, wat_ref[...], preferred_element_type=F32)
    qk = ua[:, :QK_COLS]
    ms = jnp.dot((qk * qk).astype(BF16), ones_ref[...], preferred_element_type=F32) * (1.0 / HEAD_DIM)
    qn = qk * lax.rsqrt(ms + NORM_EPS) * qkg_ref[...]
    half = ROPE_AXIS_DIM // 2
    rot = []
    for c in range(QK_COLS // LANES):
        xc = qn[:, c * LANES:(c + 1) * LANES]
        lane = lax.broadcasted_iota(jnp.int32, xc.shape, 1)
        rot.append(jnp.where((lane % ROPE_AXIS_DIM) < half, pltpu.roll(xc, LANES - half, 1), pltpu.roll(xc, half, 1)))
    rot = jnp.concatenate(rot, axis=1)
    qk_ref[...] = (qn * cos_ref[...] + rot * sin_ref[...]).astype(BF16)
    v_ref[...] = ua[:, QK_COLS:].astype(BF16)


def _inproj(x, g, wna, wml, wg, wcv, wat, ones_qk, qkg, cos_t, sin_t, seq):
    n_tok = x.shape[0]
    tm = TOKEN_TILE
    tiles_per_seq = seq // tm
    row = lambda i: (i, 0)
    pos = lambda i: (i % tiles_per_seq, 0)
    outs = [
        jax.ShapeDtypeStruct((n_tok, NA_COLS), BF16),
        jax.ShapeDtypeStruct((n_tok, 4 * BRANCH_WIDTH), BF16),
        jax.ShapeDtypeStruct((n_tok, LANES), F32),
        jax.ShapeDtypeStruct((n_tok, CONV_COLS), BF16),
        jax.ShapeDtypeStruct((n_tok, QK_COLS), BF16),
        jax.ShapeDtypeStruct((n_tok, ATT_KV_HEADS * HEAD_DIM), BF16),
    ]
    return pl.pallas_call(
        _inproj_body,
        grid=(n_tok // tm,),
        in_specs=[
            pl.BlockSpec((tm, D_MODEL), row),
            _const_spec((1, D_MODEL)),
            _const_spec(wna.shape), _const_spec(wml.shape), _const_spec(wg.shape), _const_spec(wcv.shape),
            _const_spec(wat.shape), _const_spec(ones_qk.shape), _const_spec(qkg.shape),
            pl.BlockSpec((tm, QK_COLS), pos), pl.BlockSpec((tm, QK_COLS), pos),
        ],
        out_specs=[pl.BlockSpec((tm, o.shape[1]), row) for o in outs],
        out_shape=outs,
        compiler_params=_cparams(("parallel",), 48),
        name="inproj",
    )(x, g, wna, wml, wg, wcv, wat, ones_qk, qkg, cos_t, sin_t)


def _block_diag_mask(rows, cols):
    r = lax.broadcasted_iota(jnp.int32, (rows, cols), 0) // HEAD_DIM
    c = lax.broadcasted_iota(jnp.int32, (rows, cols), 1) // HEAD_DIM
    return r == (c % N_HEADS)


def _tile_heads(x):
    t = jnp.concatenate([x] * N_HEADS, axis=0)
    return jnp.where(_block_diag_mask(N_HEADS * HEAD_DIM, BRANCH_WIDTH), t, jnp.zeros_like(t))


def _na_body(q_ref, k_ref, v_ref, bias_ref, o_ref, *, rows):
    n_keys = NA_KH * GRID_W

    def one_row(r, carry):
        r0 = jnp.clip(r - NA_KH // 2, 0, rows - NA_KH)
        delta = r0 - r + NA_KH - 1
        q = q_ref[pl.ds(pl.multiple_of(r * GRID_W, GRID_W), GRID_W), :]
        kk = k_ref[pl.ds(pl.multiple_of(r0 * GRID_W, GRID_W), n_keys), :]
        vv = v_ref[pl.ds(pl.multiple_of(r0 * GRID_W, GRID_W), n_keys), :]
        s = lax.dot_general(_tile_heads(q), kk, (((1,), (1,)), ((), ())), preferred_element_type=F32)
        s = s + bias_ref[delta]
        m = jnp.max(s, axis=1, keepdims=True)
        p = jnp.exp(s - m)
        l = jnp.sum(p, axis=1, keepdims=True)
        of = jnp.dot(p.astype(BF16), vv, preferred_element_type=F32) / l
        of = jnp.where(_block_diag_mask(N_HEADS * HEAD_DIM, BRANCH_WIDTH), of, 0.0)
        o = of[0:HEAD_DIM] + of[HEAD_DIM:2 * HEAD_DIM] + of[2 * HEAD_DIM:3 * HEAD_DIM] + of[3 * HEAD_DIM:]
        o_ref[pl.ds(pl.multiple_of(r * GRID_W, GRID_W), GRID_W), :] = o.astype(BF16)
        return carry

    lax.fori_loop(0, rows, one_row, 0)


def _na_bias_table(rpb):
    qc = np.arange(GRID_W)[:, None]
    kc = np.arange(GRID_W)[None, :]
    win = np.clip(qc - NA_KW // 2, 0, GRID_W - NA_KW)
    in_win = (kc >= win) & (kc < win + NA_KW)
    col_off = np.clip(kc - qc + NA_KW - 1, 0, 2 * NA_KW - 2)
    n_ro, n_co = 2 * NA_KH - 1, 2 * NA_KW - 1
    pick = np.zeros((n_co, GRID_W * GRID_W), np.float32)
    pick[col_off.reshape(-1), np.arange(GRID_W * GRID_W)] = 1.0
    c = jnp.dot(rpb.astype(F32).reshape(N_HEADS * n_ro, n_co), pick, precision=lax.Precision.HIGHEST)
    c = jnp.where(in_win.reshape(1, -1), c, NEG_INF).reshape(N_HEADS, n_ro, GRID_W, GRID_W)
    c = c.transpose(0, 2, 1, 3)
    b = jnp.stack([c[:, :, d:d + NA_KH, :] for d in range(NA_KH)], axis=0)
    return b.reshape(NA_KH, N_HEADS * GRID_W, NA_KH * GRID_W)


def _na(u_na, bias, batch, seq):
    rows = seq // GRID_W
    return pl.pallas_call(
        functools.partial(_na_body, rows=rows),
        grid=(batch,),
        in_specs=[
            pl.BlockSpec((seq, BRANCH_WIDTH), lambda b: (b, 0)),
            pl.BlockSpec((seq, BRANCH_WIDTH), lambda b: (b, 1)),
            pl.BlockSpec((seq, BRANCH_WIDTH), lambda b: (b, 2)),
            _const_spec(bias.shape),
        ],
        out_specs=pl.BlockSpec((seq, BRANCH_WIDTH), lambda b: (b, 0)),
        out_shape=jax.ShapeDtypeStruct((batch * seq, BRANCH_WIDTH), BF16),
        compiler_params=_cparams(("parallel",), 48),
        name="na",
    )(u_na, u_na, u_na, bias)


def _expand_heads(cols):
    hid = lax.broadcasted_iota(jnp.int32, (ML_CHUNK, BRANCH_WIDTH), 1) // HEAD_DIM
    out = jnp.broadcast_to(cols[:, 0:1], (ML_CHUNK, BRANCH_WIDTH))
    for h in range(1, N_HEADS):
        out = jnp.where(hid == h, cols[:, h:h + 1], out)
    return out


def _head_rowmax(x):
    hid = lax.broadcasted_iota(jnp.int32, x.shape, 1) // HEAD_DIM
    out = jnp.zeros_like(x)
    for h in range(N_HEADS):
        mh = jnp.max(jnp.where(hid == h, x, NEG_INF), axis=1, keepdims=True)
        out = jnp.where(hid == h, mh, out)
    return out


def _mlstm_step_fn(reverse, row0, q_ref, k_ref, v_ref, gates_ref, gb_ref, h_ref, state_ref, n_chunks):
    L = ML_CHUNK
    W = BRANCH_WIDTH
    sub = lax.broadcasted_iota(jnp.int32, (L, W), 0)
    pos = lax.broadcasted_iota(jnp.int32, (L, W), 1) % HEAD_DIM
    diag = sub == pos
    causal = (pos >= sub) if reverse else (pos <= sub)
    tr = lax.broadcasted_iota(jnp.int32, (L, L), 0)
    tc = lax.broadcasted_iota(jnp.int32, (L, L), 1)
    tri = ((tc >= tr) if reverse else (tc <= tr)).astype(F32)
    ones_bd = _block_diag_mask(W, W).astype(BF16)
    bd2 = _block_diag_mask(W, 2 * W)
    i_off, f_off = (2 * N_HEADS, 3 * N_HEADS) if reverse else (0, N_HEADS)
    state_ref[...] = jnp.zeros_like(state_ref)

    def step(ci, m_run):
        c = (n_chunks - 1 - ci) if reverse else ci
        sl = pl.ds(pl.multiple_of(row0 + c * L, L), L)
        q = q_ref[sl, :]
        k = k_ref[sl, :]
        v = v_ref[sl, :]
        g = gates_ref[sl, :] + gb_ref[...]
        i_e = _expand_heads(g[:, i_off:i_off + N_HEADS])
        f_e = _expand_heads(jax.nn.log_sigmoid(g[:, f_off:f_off + N_HEADS]))
        b = jnp.dot(tri, f_e, precision=lax.Precision.HIGHEST, preferred_element_type=F32)
        b_end = b[0:1, :] if reverse else b[L - 1:L, :]
        rrow = jnp.sum(jnp.where(diag, i_e - b, 0.0), axis=0, keepdims=True)
        log_d = jnp.where(causal, b + rrow, NEG_INF)
        inter = b + m_run
        m_row = jnp.maximum(inter, _head_rowmax(log_d))
        d = jnp.exp(log_d - m_row)
        w_inter = jnp.exp(inter - m_row)
        s = lax.dot_general(q, _tile_heads(k), (((1,), (1,)), ((), ())), preferred_element_type=F32)
        p = (s * d).astype(BF16)
        r_loc = jnp.dot(p, jnp.concatenate([_tile_heads(v), ones_bd], axis=1), preferred_element_type=F32)
        r_int = jnp.dot(q, state_ref[...].astype(BF16), preferred_element_type=F32)
        num = r_loc[:, :W] + w_inter * r_int[:, :W]
        den = r_loc[:, W:] + w_inter * r_int[:, W:]
        h_ref[sl, :] = num / jnp.maximum(jnp.abs(den), jnp.exp(-m_row))
        log_w = b_end - b + i_e
        m_new = jnp.maximum(b_end + m_run, jnp.max(log_w, axis=0, keepdims=True))
        w = jnp.exp(log_w - m_new)
        decay = jnp.exp(b_end + m_run - m_new)
        upd_rhs = jnp.concatenate([(w * v.astype(F32)).astype(BF16), w.astype(BF16)], axis=1)
        upd = lax.dot_general(k, upd_rhs, (((0,), (0,)), ((), ())), preferred_element_type=F32)
        upd = jnp.where(bd2, upd, 0.0)
        state_ref[...] = jnp.concatenate([decay, decay], axis=1) * state_ref[...] + upd
        return m_new

    return step


ML_ROWS_PER_STEP = 8192


def _mlstm_body(q_ref, k_ref, v_ref, og_ref, gates_ref, gb_ref, ng_ref, o_ref, hf_ref, hb_ref, state_ref, *, seq, n_seq):
    n_chunks = seq // ML_CHUNK
    steps = []
    for j in range(n_seq):
        for reverse in (False, True):
            steps.append(_mlstm_step_fn(reverse, j * seq, q_ref, k_ref, v_ref, gates_ref, gb_ref,
                                        hb_ref if reverse else hf_ref, state_ref.at[2 * j + int(reverse)], n_chunks))
    m0 = jnp.full((1, BRANCH_WIDTH), NEG_INF, F32)
    lax.fori_loop(0, n_chunks, lambda ci, ms: tuple(step(ci, m) for step, m in zip(steps, ms)), (m0,) * len(steps))
    ones_bd = _block_diag_mask(BRANCH_WIDTH, BRANCH_WIDTH).astype(BF16)
    tile = 256

    def finish(t, carry):
        sl = pl.ds(pl.multiple_of(t * tile, tile), tile)
        hs = hf_ref[sl, :] + hb_ref[sl, :]
        ms = jnp.dot((hs * hs).astype(BF16), ones_bd, preferred_element_type=F32) * (1.0 / HEAD_DIM)
        hm = hs * lax.rsqrt(ms + NORM_EPS) * ng_ref[...]
        o_ref[sl, :] = (hm * _sigmoid(og_ref[sl, :].astype(F32))).astype(BF16)
        return carry

    lax.fori_loop(0, n_seq * seq // tile, finish, 0)


def _mlstm(u_ml, gates, gate_b, norm_g, batch, seq):
    n_seq = min(batch, ML_ROWS_PER_STEP // seq)
    rows = n_seq * seq
    col = lambda j: pl.BlockSpec((rows, BRANCH_WIDTH), lambda b: (b, j), pipeline_mode=pl.Buffered(1))
    return pl.pallas_call(
        functools.partial(_mlstm_body, seq=seq, n_seq=n_seq),
        grid=(batch // n_seq,),
        in_specs=[col(0), col(1), col(2), col(3),
                  pl.BlockSpec((rows, LANES), lambda b: (b, 0), pipeline_mode=pl.Buffered(1)),
                  _const_spec((1, LANES)), _const_spec((1, BRANCH_WIDTH))],
        out_specs=pl.BlockSpec((rows, BRANCH_WIDTH), lambda b: (b, 0)),
        out_shape=jax.ShapeDtypeStruct((batch * seq, BRANCH_WIDTH), BF16),
        scratch_shapes=[pltpu.VMEM((rows, BRANCH_WIDTH), F32), pltpu.VMEM((rows, BRANCH_WIDTH), F32),
                        pltpu.VMEM((2 * n_seq, BRANCH_WIDTH, 2 * BRANCH_WIDTH), F32)],
        compiler_params=_cparams(("parallel",), 56),
        name="mlstm",
    )(u_ml, u_ml, u_ml, u_ml, gates, gate_b, norm_g)


CONV_PAD = 16
CONV_TILE = 64


def _conv_body(u_ref, w_ref, b_ref, g_ref, beta_ref, o_ref, pad_ref, *, seq):
    u = u_ref[...]
    a = u[:, :BRANCH_WIDTH].astype(F32)
    gate = u[:, BRANCH_WIDTH:].astype(F32)
    zeros = jnp.zeros((CONV_PAD, BRANCH_WIDTH), F32)
    pad_ref[0:CONV_PAD, :] = zeros
    pad_ref[CONV_PAD + seq:, :] = zeros
    pad_ref[CONV_PAD:CONV_PAD + seq, :] = a * _sigmoid(gate)
    first = CONV_PAD - CONV_WIDTH // 2
    span = CONV_TILE + 2 * CONV_PAD - SUBLANES

    def tile(t, carry):
        base = pl.multiple_of(t * CONV_TILE, CONV_TILE)
        halves = []
        for c in range(BRANCH_WIDTH // LANES):
            lanes = slice(c * LANES, (c + 1) * LANES)
            win = pad_ref[pl.ds(base, CONV_TILE + 2 * CONV_PAD), lanes]
            n_blk = CONV_TILE // SUBLANES
            acc = jnp.zeros((n_blk, SUBLANES, LANES), F32)
            for r in range(SUBLANES):
                shifted = win[r:r + span, :].reshape(span // SUBLANES, SUBLANES, LANES)
                for j in range(CONV_WIDTH):
                    a8, rj = divmod(first + j, SUBLANES)
                    if rj == r:
                        acc = acc + shifted[a8:a8 + n_blk] * w_ref[j, :, lanes][None]
            halves.append(acc.reshape(CONV_TILE, LANES))
        y = jnp.concatenate(halves, axis=1) + b_ref[...]
        mu = jnp.mean(y, axis=-1, keepdims=True)
        yc = y - mu
        var = jnp.mean(yc * yc, axis=-1, keepdims=True)
        z = yc * lax.rsqrt(var + NORM_EPS) * g_ref[...] + beta_ref[...]
        o_ref[pl.ds(base, CONV_TILE), :] = (z * _sigmoid(z)).astype(BF16)
        return carry

    lax.fori_loop(0, seq // CONV_TILE, tile, 0)


def _conv(u_cv, w, b, g, beta, batch, seq):
    return pl.pallas_call(
        functools.partial(_conv_body, seq=seq),
        grid=(batch,),
        in_specs=[pl.BlockSpec((seq, CONV_COLS), lambda i: (i, 0)),
                  _const_spec(w.shape), _const_spec(b.shape), _const_spec(g.shape), _const_spec(beta.shape)],
        out_specs=pl.BlockSpec((seq, BRANCH_WIDTH), lambda i: (i, 0)),
        out_shape=jax.ShapeDtypeStruct((batch * seq, BRANCH_WIDTH), BF16),
        scratch_shapes=[pltpu.VMEM((seq + 2 * CONV_PAD, BRANCH_WIDTH), F32)],
        compiler_params=_cparams(("parallel",), 32),
        name="conv",
    )(u_cv, w, b, g, beta)


ATT_Q_TILE = 256


def _gqa_body(q_ref, k_ref, v_ref, o_ref):
    k = k_ref[...]
    v = v_ref[...]
    lane = lax.broadcasted_iota(jnp.int32, (ATT_Q_TILE, LANES), 1)
    low = lane < HEAD_DIM
    for c in range(2):
        qc = q_ref[:, c * LANES:(c + 1) * LANES]
        halves = []
        for keep in (low, jnp.logical_not(low)):
            qh = jnp.where(keep, qc, jnp.zeros_like(qc))
            s = lax.dot_general(qh, k, (((1,), (1,)), ((), ())), preferred_element_type=F32)
            m = jnp.max(s, axis=1, keepdims=True)
            p = jnp.exp(s - m)
            l = jnp.sum(p, axis=1, keepdims=True)
            halves.append(jnp.dot(p.astype(BF16), v, preferred_element_type=F32) / l)
        o_ref[:, c * LANES:(c + 1) * LANES] = jnp.where(low, halves[0], halves[1]).astype(BF16)


def _gqa(qk, v, batch, seq):
    tiles = seq // ATT_Q_TILE
    return pl.pallas_call(
        _gqa_body,
        grid=(batch, tiles),
        in_specs=[pl.BlockSpec((ATT_Q_TILE, N_HEADS * HEAD_DIM), lambda b, i: (b * tiles + i, 0)),
                  pl.BlockSpec((seq, LANES), lambda b, i: (b, 2)),
                  pl.BlockSpec((seq, LANES), lambda b, i: (b, 0))],
        out_specs=pl.BlockSpec((ATT_Q_TILE, BRANCH_WIDTH), lambda b, i: (b * tiles + i, 0)),
        out_shape=jax.ShapeDtypeStruct((batch * seq, BRANCH_WIDTH), BF16),
        compiler_params=_cparams(("parallel", "parallel"), 48),
        name="gqa",
    )(qk, qk, v)


def _sigmoid(x):
    return 0.5 * jnp.tanh(0.5 * x) + 0.5


def _merge_body(x_ref, g_ref, b0_ref, b1_ref, b2_ref, b3_ref, wg_ref, bg_ref, wb_ref, wo_ref, o_ref):
    x = x_ref[...]
    h = _rms(x, g_ref[...]).astype(BF16)
    acc = jnp.zeros(x.shape, F32)
    for n, br_ref in enumerate((b0_ref, b1_ref, b2_ref, b3_ref)):
        cols = slice(n * D_MODEL, (n + 1) * D_MODEL)
        gl = jnp.dot(h, wg_ref[:, cols], preferred_element_type=F32) + bg_ref[:, cols]
        pr = jnp.dot(br_ref[...], wb_ref[n], preferred_element_type=F32)
        acc = acc + _sigmoid(gl) * pr
    o_ref[...] = x + jnp.dot(acc.astype(BF16), wo_ref[...], preferred_element_type=F32)


def _merge(x, g, branches, w_gate, b_gate, w_branch, w_out):
    n_tok = x.shape[0]
    tm = TOKEN_TILE
    row = lambda i: (i, 0)
    return pl.pallas_call(
        _merge_body,
        grid=(n_tok // tm,),
        in_specs=[pl.BlockSpec((tm, D_MODEL), row), _const_spec((1, D_MODEL))]
        + [pl.BlockSpec((tm, BRANCH_WIDTH), row)] * N_BRANCHES
        + [_const_spec(w_gate.shape), _const_spec(b_gate.shape), _const_spec(w_branch.shape), _const_spec(w_out.shape)],
        out_specs=pl.BlockSpec((tm, D_MODEL), row),
        out_shape=jax.ShapeDtypeStruct((n_tok, D_MODEL), F32),
        compiler_params=_cparams(("parallel",), 56),
        name="merge",
    )(x, g, *branches, w_gate, b_gate, w_branch, w_out)


MOE_TOKEN_TILE = 1024


def _split_bf16(x):
    hi = x.astype(BF16)
    return hi, (x - hi.astype(F32)).astype(BF16)


def _router(h, wr_hi_ref, wr_lo_ref, br_ref):
    h_hi, h_lo = _split_bf16(h)
    logits = (jnp.dot(h_hi, wr_hi_ref[...], preferred_element_type=F32)
              + jnp.dot(h_lo, wr_hi_ref[...], preferred_element_type=F32)
              + jnp.dot(h_hi, wr_lo_ref[...], preferred_element_type=F32)) + br_ref[...]
    lane = lax.broadcasted_iota(jnp.int32, logits.shape, 1)
    big = jnp.int32(ROUTER_LANES)
    is_g = lane < MOE_GROUPS
    gl = jnp.where(is_g, logits, NEG_INF)
    g_max = jnp.max(gl, axis=1, keepdims=True)
    g_idx = jnp.min(jnp.where(is_g & (gl == g_max), lane, big), axis=1, keepdims=True)
    g_gate = 1.0 / jnp.sum(jnp.where(is_g, jnp.exp(gl - g_max), 0.0), axis=1, keepdims=True)
    e_lo = ROUTER_E0 + g_idx * MOE_EXPERTS_PER_GROUP
    in_grp = (lane >= e_lo) & (lane < e_lo + MOE_EXPERTS_PER_GROUP)
    el = jnp.where(in_grp, logits, NEG_INF)
    top1 = jnp.max(el, axis=1, keepdims=True)
    idx1 = jnp.min(jnp.where(in_grp & (el == top1), lane, big), axis=1, keepdims=True)
    rest = in_grp & (lane != idx1)
    el2 = jnp.where(rest, logits, NEG_INF)
    top2 = jnp.max(el2, axis=1, keepdims=True)
    idx2 = jnp.min(jnp.where(rest & (el2 == top2), lane, big), axis=1, keepdims=True)
    e2 = jnp.exp(top2 - top1)
    w1 = g_gate / (1.0 + e2)
    w2 = g_gate * e2 / (1.0 + e2)
    return jnp.where(lane == idx1, w1, 0.0) + jnp.where(lane == idx2, w2, 0.0)


def _moe_body(x_ref, g_ref, wrh_ref, wrl_ref, br_ref, w1_ref, w3_ref, w2_ref, fg_ref, o_ref, h_ref, comb_ref, acc_ref, *,
              final):
    e = pl.program_id(1)

    @pl.when(e == 0)
    def _():
        h = _rms(x_ref[...], g_ref[...])
        h_ref[...] = h.astype(BF16)
        comb_ref[...] = _router(h, wrh_ref, wrl_ref, br_ref)
        acc_ref[...] = jnp.zeros_like(acc_ref)

    h = h_ref[...]
    a = jnp.dot(h, w1_ref[0], preferred_element_type=F32)
    b = jnp.dot(h, w3_ref[0], preferred_element_type=F32)
    comb = comb_ref[...]
    lane = lax.broadcasted_iota(jnp.int32, comb.shape, 1)
    c = jnp.sum(jnp.where(lane == ROUTER_E0 + e, comb, 0.0), axis=1, keepdims=True)
    hid = (a * _sigmoid(a) * b * c).astype(BF16)
    acc_ref[...] += jnp.dot(hid, w2_ref[0], preferred_element_type=F32)

    @pl.when(e == MOE_N_EXPERTS - 1)
    def _():
        y = x_ref[...] + acc_ref[...]
        o_ref[...] = _rms(y, fg_ref[...]) if final else y


def _moe(x, g, wr_hi, wr_lo, br, w1, w3, w2, final_g, final):
    n_tok = x.shape[0]
    tm = MOE_TOKEN_TILE
    row = lambda i, e: (i, 0)
    exp = lambda i, e: (e, 0, 0)
    return pl.pallas_call(
        functools.partial(_moe_body, final=final),
        grid=(n_tok // tm, MOE_N_EXPERTS),
        in_specs=[pl.BlockSpec((tm, D_MODEL), row), _const_spec((1, D_MODEL)),
                  _const_spec(wr_hi.shape), _const_spec(wr_lo.shape), _const_spec(br.shape),
                  pl.BlockSpec((1, D_MODEL, MOE_D_EXPERT), exp), pl.BlockSpec((1, D_MODEL, MOE_D_EXPERT), exp),
                  pl.BlockSpec((1, MOE_D_EXPERT, D_MODEL), exp), _const_spec((1, D_MODEL))],
        out_specs=pl.BlockSpec((tm, D_MODEL), row),
        out_shape=jax.ShapeDtypeStruct((n_tok, D_MODEL), F32),
        scratch_shapes=[pltpu.VMEM((tm, D_MODEL), BF16), pltpu.VMEM((tm, ROUTER_LANES), F32),
                        pltpu.VMEM((tm, D_MODEL), F32)],
        compiler_params=_cparams(("parallel", "arbitrary"), 48),
        name="moe",
    )(x, g, wr_hi, wr_lo, br, w1, w3, w2, final_g)


def _rope_tables(seq):
    t = jnp.arange(seq)
    pos = jnp.stack([t // GRID_W, t % GRID_W], axis=-1).astype(F32)
    inv = ROPE_THETA ** (-jnp.arange(0, ROPE_AXIS_DIM, 2, dtype=F32) / ROPE_AXIS_DIM)
    ang = pos[..., None] * inv
    ang = jnp.concatenate([ang, ang], axis=-1).reshape(seq, HEAD_DIM)
    first = (jnp.arange(HEAD_DIM) % ROPE_AXIS_DIM) < (ROPE_AXIS_DIM // 2)
    cos = jnp.cos(ang)
    sin = jnp.where(first[None, :], -jnp.sin(ang), jnp.sin(ang))
    reps = QK_COLS // HEAD_DIM
    return jnp.tile(cos, (1, reps)), jnp.tile(sin, (1, reps))


ATT_HEAD_ORDER = (0, 2, 1, 3)


def _pack_layer(l, w_in, na_rpb, ml_gate_b, ml_norm_g, conv_dw_w, conv_dw_b, conv_norm_g, conv_norm_b,
                att_q_norm_g, att_k_norm_g, w_branch, w_gate, b_gate, w_out, moe_w_group, moe_b_group,
                moe_w_expert, moe_b_expert, moe_w1, moe_w3, moe_w2):
    scale = HEAD_DIM ** -0.5
    w = w_in[l]
    o0 = 0
    wna = w[:, o0:o0 + NA_COLS]
    wna = jnp.concatenate([wna[:, :BRANCH_WIDTH] * scale, wna[:, BRANCH_WIDTH:]], axis=1)
    o0 += NA_COLS
    wml = w[:, o0:o0 + 4 * BRANCH_WIDTH]
    wml = jnp.concatenate([wml[:, :BRANCH_WIDTH], wml[:, BRANCH_WIDTH:2 * BRANCH_WIDTH] * scale,
                           wml[:, 2 * BRANCH_WIDTH:]], axis=1)
    o0 += 4 * BRANCH_WIDTH
    n_gates = ML_N_GATES * N_HEADS
    wg = jnp.pad(w[:, o0:o0 + n_gates], ((0, 0), (0, LANES - n_gates)))
    o0 += n_gates
    wcv = w[:, o0:o0 + CONV_COLS]
    o0 += CONV_COLS
    wat = w[:, o0:o0 + ATT_COLS]
    wq = wat[:, :N_HEADS * HEAD_DIM].reshape(D_MODEL, N_HEADS, HEAD_DIM)[:, ATT_HEAD_ORDER, :]
    wat = jnp.concatenate([wq.reshape(D_MODEL, N_HEADS * HEAD_DIM), wat[:, N_HEADS * HEAD_DIM:]], axis=1)
    qkg = jnp.concatenate([jnp.tile(att_q_norm_g[l] * scale, N_HEADS), jnp.tile(att_k_norm_g[l], ATT_KV_HEADS)])
    wb = w_branch[l]
    wb3 = wb[3].reshape(N_HEADS, HEAD_DIM, D_MODEL)[ATT_HEAD_ORDER, :, :].reshape(BRANCH_WIDTH, D_MODEL)
    wb = jnp.concatenate([wb[:3], wb3[None]], axis=0)
    wr = jnp.concatenate([moe_w_group[l], moe_w_expert[l].reshape(D_MODEL, MOE_N_EXPERTS)], axis=1)
    n_r = MOE_GROUPS + MOE_N_EXPERTS
    br = jnp.concatenate([moe_b_group[l], moe_b_expert[l].reshape(MOE_N_EXPERTS)])
    wr_hi, wr_lo = _split_bf16(jnp.pad(wr, ((0, 0), (0, ROUTER_LANES - n_r))))
    return dict(
        wna=wna.astype(BF16), wml=wml.astype(BF16), wg=wg.astype(BF16), wcv=wcv.astype(BF16), wat=wat.astype(BF16),
        qkg=qkg.reshape(1, QK_COLS).astype(F32),
        na_bias=_na_bias_table(na_rpb[l]),
        gate_b=jnp.pad(ml_gate_b[l].reshape(1, n_gates), ((0, 0), (0, LANES - n_gates))).astype(F32),
        ml_norm_g=ml_norm_g[l].reshape(1, BRANCH_WIDTH),
        conv_w=jnp.broadcast_to(conv_dw_w[l][:, None, :], (CONV_WIDTH, SUBLANES, BRANCH_WIDTH)), conv_b=conv_dw_b[l].reshape(1, BRANCH_WIDTH),
        conv_g=conv_norm_g[l].reshape(1, BRANCH_WIDTH), conv_beta=conv_norm_b[l].reshape(1, BRANCH_WIDTH),
        w_gate=w_gate[l].astype(BF16), b_gate=b_gate[l].reshape(1, N_BRANCHES * D_MODEL),
        w_branch=wb.astype(BF16), w_out=w_out[l].astype(BF16),
        wr_hi=wr_hi, wr_lo=wr_lo, br=jnp.pad(br, (0, ROUTER_LANES - n_r)).reshape(1, ROUTER_LANES),
        w1=moe_w1[l].astype(BF16), w3=moe_w3[l].astype(BF16), w2=moe_w2[l].astype(BF16),
    )


def _trunk(x, layers, norm_mix_g, norm_ffn_g, final_norm_g, ones_qk, depth):
    batch, seq, _ = x.shape
    xf = x.reshape(batch * seq, D_MODEL)
    cos_t, sin_t = _rope_tables(seq)
    fg = final_norm_g.reshape(1, D_MODEL)
    for l in range(depth):
        p = layers[l]
        gm = norm_mix_g[l].reshape(1, D_MODEL)
        u_na, u_ml, gates, u_cv, qk, v = _inproj(xf, gm, p["wna"], p["wml"], p["wg"], p["wcv"], p["wat"], ones_qk,
                                                 p["qkg"], cos_t, sin_t, seq)
        o_na = _na(u_na, p["na_bias"], batch, seq)
        o_ml = _mlstm(u_ml, gates, p["gate_b"], p["ml_norm_g"], batch, seq)
        o_cv = _conv(u_cv, p["conv_w"], p["conv_b"], p["conv_g"], p["conv_beta"], batch, seq)
        o_at = _gqa(qk, v, batch, seq)
        xf = _merge(xf, gm, (o_na, o_ml, o_cv, o_at), p["w_gate"], p["b_gate"], p["w_branch"], p["w_out"])
        xf = _moe(xf, norm_ffn_g[l].reshape(1, D_MODEL), p["wr_hi"], p["wr_lo"], p["br"], p["w1"], p["w3"], p["w2"], fg,
                  final=(l == depth - 1))
    return xf.reshape(batch, seq, D_MODEL)


def kernel(x_prompt, x_sample, norm_mix_g, w_in, na_rpb, ml_gate_b, ml_norm_g, conv_dw_w, conv_dw_b, conv_norm_g, conv_norm_b, att_q_norm_g, att_k_norm_g, w_branch, w_gate, b_gate, w_out, norm_ffn_g, moe_w_group, moe_b_group, moe_w_expert, moe_b_expert, moe_w1, moe_w3, moe_w2, final_norm_g):
    depth = w_in.shape[0]
    layers = [_pack_layer(l, w_in, na_rpb, ml_gate_b, ml_norm_g, conv_dw_w, conv_dw_b, conv_norm_g, conv_norm_b,
                          att_q_norm_g, att_k_norm_g, w_branch, w_gate, b_gate, w_out, moe_w_group, moe_b_group,
                          moe_w_expert, moe_b_expert, moe_w1, moe_w3, moe_w2) for l in range(depth)]
    hid = np.arange(QK_COLS) // HEAD_DIM
    ones_qk = jnp.asarray(hid[:, None] == hid[None, :], BF16)
    y_prompt = _trunk(x_prompt, layers, norm_mix_g, norm_ffn_g, final_norm_g, ones_qk, depth)
    y_sample = _trunk(x_sample, layers, norm_mix_g, norm_ffn_g, final_norm_g, ones_qk, depth)
    return (y_prompt, y_sample)
```

```python
import functools

import jax
import jax.numpy as jnp
import numpy as np
from jax import lax
from jax.experimental import pallas as pl
from jax.experimental.pallas import tpu as pltpu

F32 = jnp.float32
BF16 = jnp.bfloat16

D_MODEL = 1024
GRID_W = 64
HEAD_DIM = 64
BRANCH_WIDTH = 256
N_BRANCHES = 4
N_HEADS = 4
NA_KH = 8
NA_KW = 16
ML_CHUNK = 64
ML_N_GATES = 4
CONV_WIDTH = 31
ATT_KV_HEADS = 2
ROPE_THETA = 10000.0
ROPE_AXIS_DIM = HEAD_DIM // 2
MOE_GROUPS = 4
MOE_EXPERTS_PER_GROUP = 4
MOE_N_EXPERTS = 16
MOE_D_EXPERT = 256
NORM_EPS = 1e-6
NEG_INF = -1e30

NA_COLS = 3 * BRANCH_WIDTH
ML_COLS = 4 * BRANCH_WIDTH + ML_N_GATES * N_HEADS
CONV_COLS = 2 * BRANCH_WIDTH
ATT_COLS = (N_HEADS + 2 * ATT_KV_HEADS) * HEAD_DIM
QK_COLS = (N_HEADS + ATT_KV_HEADS) * HEAD_DIM

LANES = 128
SUBLANES = 8
ROUTER_LANES = LANES
ROUTER_E0 = MOE_GROUPS
V7X_VMEM_BYTES = 64 * 1024 * 1024

TOKEN_TILE = 512


def _cparams(semantics, vmem_mb):
    assert vmem_mb * 1024 * 1024 < V7X_VMEM_BYTES
    return pltpu.CompilerParams(dimension_semantics=semantics, vmem_limit_bytes=vmem_mb * 1024 * 1024)


def _const_spec(shape):
    nd = len(shape)
    return pl.BlockSpec(shape, lambda *_: (0,) * nd)


def _rms(x, g):
    ms = jnp.mean(x * x, axis=-1, keepdims=True)
    return x * lax.rsqrt(ms + NORM_EPS) * g


def _inproj_body(x_ref, g_ref, wna_ref, wml_ref, wg_ref, wcv_ref, wat_ref, ones_ref, qkg_ref, cos_ref, sin_ref,
                 una_ref, uml_ref, gates_ref, ucv_ref, qk_ref, v_ref):
    h = _rms(x_ref[...], g_ref[...]).astype(BF16)
    una_ref[...] = jnp.dot(h, wna_ref[...], preferred_element_type=F32).astype(BF16)
    uml_ref[...] = jnp.dot(h, wml_ref[...], preferred_element_type=F32).astype(BF16)
    gates_ref[...] = jnp.dot(h, wg_ref[...], preferred_element_type=F32)
    ucv_ref[...] = jnp.dot(h, wcv_ref[...], preferred_element_type=F32).astype(BF16)
    ua = jnp.dot(h, wat_ref[...], preferred_element_type=F32)
    qk = ua[:, :QK_COLS]
    ms = jnp.dot((qk * qk).astype(BF16), ones_ref[...], preferred_element_type=F32) * (1.0 / HEAD_DIM)
    qn = qk * lax.rsqrt(ms + NORM_EPS) * qkg_ref[...]
    half = ROPE_AXIS_DIM // 2
    rot = []
    for c in range(QK_COLS // LANES):
        xc = qn[:, c * LANES:(c + 1) * LANES]
        lane = lax.broadcasted_iota(jnp.int32, xc.shape, 1)
        rot.append(jnp.where((lane % ROPE_AXIS_DIM) < half, pltpu.roll(xc, LANES - half, 1), pltpu.roll(xc, half, 1)))
    rot = jnp.concatenate(rot, axis=1)
    qk_ref[...] = (qn * cos_ref[...] + rot * sin_ref[...]).astype(BF16)
    v_ref[...] = ua[:, QK_COLS:].astype(BF16)


def _inproj(x, g, wna, wml, wg, wcv, wat, ones_qk, qkg, cos_t, sin_t, seq):
    n_tok = x.shape[0]
    tm = TOKEN_TILE
    tiles_per_seq = seq // tm
    row = lambda i: (i, 0)
    pos = lambda i: (i % tiles_per_seq, 0)
    outs = [
        jax.ShapeDtypeStruct((n_tok, NA_COLS), BF16),
        jax.ShapeDtypeStruct((n_tok, 4 * BRANCH_WIDTH), BF16),
        jax.ShapeDtypeStruct((n_tok, LANES), F32),
        jax.ShapeDtypeStruct((n_tok, CONV_COLS), BF16),
        jax.ShapeDtypeStruct((n_tok, QK_COLS), BF16),
        jax.ShapeDtypeStruct((n_tok, ATT_KV_HEADS * HEAD_DIM), BF16),
    ]
    return pl.pallas_call(
        _inproj_body,
        grid=(n_tok // tm,),
        in_specs=[
            pl.BlockSpec((tm, D_MODEL), row),
            _const_spec((1, D_MODEL)),
            _const_spec(wna.shape), _const_spec(wml.shape), _const_spec(wg.shape), _const_spec(wcv.shape),
            _const_spec(wat.shape), _const_spec(ones_qk.shape), _const_spec(qkg.shape),
            pl.BlockSpec((tm, QK_COLS), pos), pl.BlockSpec((tm, QK_COLS), pos),
        ],
        out_specs=[pl.BlockSpec((tm, o.shape[1]), row) for o in outs],
        out_shape=outs,
        compiler_params=_cparams(("parallel",), 48),
        name="inproj",
    )(x, g, wna, wml, wg, wcv, wat, ones_qk, qkg, cos_t, sin_t)


def _block_diag_mask(rows, cols):
    r = lax.broadcasted_iota(jnp.int32, (rows, cols), 0) // HEAD_DIM
    c = lax.broadcasted_iota(jnp.int32, (rows, cols), 1) // HEAD_DIM
    return r == (c % N_HEADS)


def _tile_heads(x):
    t = jnp.concatenate([x] * N_HEADS, axis=0)
    return jnp.where(_block_diag_mask(N_HEADS * HEAD_DIM, BRANCH_WIDTH), t, jnp.zeros_like(t))


def _na_body(q_ref, k_ref, v_ref, bias_ref, o_ref, *, rows):
    n_keys = NA_KH * GRID_W

    def one_row(r, carry):
        r0 = jnp.clip(r - NA_KH // 2, 0, rows - NA_KH)
        delta = r0 - r + NA_KH - 1
        q = q_ref[pl.ds(pl.multiple_of(r * GRID_W, GRID_W), GRID_W), :]
        kk = k_ref[pl.ds(pl.multiple_of(r0 * GRID_W, GRID_W), n_keys), :]
        vv = v_ref[pl.ds(pl.multiple_of(r0 * GRID_W, GRID_W), n_keys), :]
        s = lax.dot_general(_tile_heads(q), kk, (((1,), (1,)), ((), ())), preferred_element_type=F32)
        s = s + bias_ref[delta]
        m = jnp.max(s, axis=1, keepdims=True)
        p = jnp.exp(s - m)
        l = jnp.sum(p, axis=1, keepdims=True)
        of = jnp.dot(p.astype(BF16), vv, preferred_element_type=F32) / l
        of = jnp.where(_block_diag_mask(N_HEADS * HEAD_DIM, BRANCH_WIDTH), of, 0.0)
        o = of[0:HEAD_DIM] + of[HEAD_DIM:2 * HEAD_DIM] + of[2 * HEAD_DIM:3 * HEAD_DIM] + of[3 * HEAD_DIM:]
        o_ref[pl.ds(pl.multiple_of(r * GRID_W, GRID_W), GRID_W), :] = o.astype(BF16)
        return carry

    lax.fori_loop(0, rows, one_row, 0)


def _na_bias_table(rpb):
    qc = np.arange(GRID_W)[:, None]
    kc = np.arange(GRID_W)[None, :]
    win = np.clip(qc - NA_KW // 2, 0, GRID_W - NA_KW)
    in_win = (kc >= win) & (kc < win + NA_KW)
    col_off = np.clip(kc - qc + NA_KW - 1, 0, 2 * NA_KW - 2)
    n_ro, n_co = 2 * NA_KH - 1, 2 * NA_KW - 1
    pick = np.zeros((n_co, GRID_W * GRID_W), np.float32)
    pick[col_off.reshape(-1), np.arange(GRID_W * GRID_W)] = 1.0
    c = jnp.dot(rpb.astype(F32).reshape(N_HEADS * n_ro, n_co), pick, precision=lax.Precision.HIGHEST)
    c = jnp.where(in_win.reshape(1, -1), c, NEG_INF).reshape(N_HEADS, n_ro, GRID_W, GRID_W)
    c = c.transpose(0, 2, 1, 3)
    b = jnp.stack([c[:, :, d:d + NA_KH, :] for d in range(NA_KH)], axis=0)
    return b.reshape(NA_KH, N_HEADS * GRID_W, NA_KH * GRID_W)


def _na(u_na, bias, batch, seq):
    rows = seq // GRID_W
    return pl.pallas_call(
        functools.partial(_na_body, rows=rows),
        grid=(batch,),
        in_specs=[
            pl.BlockSpec((seq, BRANCH_WIDTH), lambda b: (b, 0)),
            pl.BlockSpec((seq, BRANCH_WIDTH), lambda b: (b, 1)),
            pl.BlockSpec((seq, BRANCH_WIDTH), lambda b: (b, 2)),
            _const_spec(bias.shape),
        ],
        out_specs=pl.BlockSpec((seq, BRANCH_WIDTH), lambda b: (b, 0)),
        out_shape=jax.ShapeDtypeStruct((batch * seq, BRANCH_WIDTH), BF16),
        compiler_params=_cparams(("parallel",), 48),
        name="na",
    )(u_na, u_na, u_na, bias)


def _expand_heads(cols):
    hid = lax.broadcasted_iota(jnp.int32, (ML_CHUNK, BRANCH_WIDTH), 1) // HEAD_DIM
    out = jnp.broadcast_to(cols[:, 0:1], (ML_CHUNK, BRANCH_WIDTH))
    for h in range(1, N_HEADS):
        out = jnp.where(hid == h, cols[:, h:h + 1], out)
    return out


def _head_rowmax(x):
    hid = lax.broadcasted_iota(jnp.int32, x.shape, 1) // HEAD_DIM
    out = jnp.zeros_like(x)
    for h in range(N_HEADS):
        mh = jnp.max(jnp.where(hid == h, x, NEG_INF), axis=1, keepdims=True)
        out = jnp.where(hid == h, mh, out)
    return out


def _mlstm_step_fn(reverse, row0, q_ref, k_ref, v_ref, gates_ref, gb_ref, h_ref, state_ref, n_chunks):
    L = ML_CHUNK
    W = BRANCH_WIDTH
    sub = lax.broadcasted_iota(jnp.int32, (L, W), 0)
    pos = lax.broadcasted_iota(jnp.int32, (L, W), 1) % HEAD_DIM
    diag = sub == pos
    causal = (pos >= sub) if reverse else (pos <= sub)
    tr = lax.broadcasted_iota(jnp.int32, (L, L), 0)
    tc = lax.broadcasted_iota(jnp.int32, (L, L), 1)
    tri = ((tc >= tr) if reverse else (tc <= tr)).astype(F32)
    ones_bd = _block_diag_mask(W, W).astype(BF16)
    bd2 = _block_diag_mask(W, 2 * W)
    i_off, f_off = (2 * N_HEADS, 3 * N_HEADS) if reverse else (0, N_HEADS)
    state_ref[...] = jnp.zeros_like(state_ref)

    def step(ci, m_run):
        c = (n_chunks - 1 - ci) if reverse else ci
        sl = pl.ds(pl.multiple_of(row0 + c * L, L), L)
        q = q_ref[sl, :]
        k = k_ref[sl, :]
        v = v_ref[sl, :]
        g = gates_ref[sl, :] + gb_ref[...]
        i_e = _expand_heads(g[:, i_off:i_off + N_HEADS])
        f_e = _expand_heads(jax.nn.log_sigmoid(g[:, f_off:f_off + N_HEADS]))
        b = jnp.dot(tri, f_e, precision=lax.Precision.HIGHEST, preferred_element_type=F32)
        b_end = b[0:1, :] if reverse else b[L - 1:L, :]
        rrow = jnp.sum(jnp.where(diag, i_e - b, 0.0), axis=0, keepdims=True)
        log_d = jnp.where(causal, b + rrow, NEG_INF)
        inter = b + m_run
        m_row = jnp.maximum(inter, _head_rowmax(log_d))
        d = jnp.exp(log_d - m_row)
        w_inter = jnp.exp(inter - m_row)
        s = lax.dot_general(q, _tile_heads(k), (((1,), (1,)), ((), ())), preferred_element_type=F32)
        p = (s * d).astype(BF16)
        r_loc = jnp.dot(p, jnp.concatenate([_tile_heads(v), ones_bd], axis=1), preferred_element_type=F32)
        r_int = jnp.dot(q, state_ref[...].astype(BF16), preferred_element_type=F32)
        num = r_loc[:, :W] + w_inter * r_int[:, :W]
        den = r_loc[:, W:] + w_inter * r_int[:, W:]
        h_ref[sl, :] = num / jnp.maximum(jnp.abs(den), jnp.exp(-m_row))
        log_w = b_end - b + i_e
        m_new = jnp.maximum(b_end + m_run, jnp.max(log_w, axis=0, keepdims=True))
        w = jnp.exp(log_w - m_new)
        decay = jnp.exp(b_end + m_run - m_new)
        upd_rhs = jnp.concatenate([(w * v.astype(F32)).astype(BF16), w.astype(BF16)], axis=1)
        upd = lax.dot_general(k, upd_rhs, (((0,), (0,)), ((), ())), preferred_element_type=F32)
        upd = jnp.where(bd2, upd, 0.0)
        state_ref[...] = jnp.concatenate([decay, decay], axis=1) * state_ref[...] + upd
        return m_new

    return step


ML_ROWS_PER_STEP = 8192


def _mlstm_body(q_ref, k_ref, v_ref, og_ref, gates_ref, gb_ref, ng_ref, o_ref, hf_ref, hb_ref, state_ref, *, seq, n_seq):
    n_chunks = seq // ML_CHUNK
    steps = []
    for j in range(n_seq):
        for reverse in (False, True):
            steps.append(_mlstm_step_fn(reverse, j * seq, q_ref, k_ref, v_ref, gates_ref, gb_ref,
                                        hb_ref if reverse else hf_ref, state_ref.at[2 * j + int(reverse)], n_chunks))
    m0 = jnp.full((1, BRANCH_WIDTH), NEG_INF, F32)
    lax.fori_loop(0, n_chunks, lambda ci, ms: tuple(step(ci, m) for step, m in zip(steps, ms)), (m0,) * len(steps))
    ones_bd = _block_diag_mask(BRANCH_WIDTH, BRANCH_WIDTH).astype(BF16)
    tile = 256

    def finish(t, carry):
        sl = pl.ds(pl.multiple_of(t * tile, tile), tile)
        hs = hf_ref[sl, :] + hb_ref[sl, :]
        ms = jnp.dot((hs * hs).astype(BF16), ones_bd, preferred_element_type=F32) * (1.0 / HEAD_DIM)
        hm = hs * lax.rsqrt(ms + NORM_EPS) * ng_ref[...]
        o_ref[sl, :] = (hm * _sigmoid(og_ref[sl, :].astype(F32))).astype(BF16)
        return carry

    lax.fori_loop(0, n_seq * seq // tile, finish, 0)


def _mlstm(u_ml, gates, gate_b, norm_g, batch, seq):
    n_seq = min(batch, ML_ROWS_PER_STEP // seq)
    rows = n_seq * seq
    col = lambda j: pl.BlockSpec((rows, BRANCH_WIDTH), lambda b: (b, j), pipeline_mode=pl.Buffered(1))
    return pl.pallas_call(
        functools.partial(_mlstm_body, seq=seq, n_seq=n_seq),
        grid=(batch // n_seq,),
        in_specs=[col(0), col(1), col(2), col(3),
                  pl.BlockSpec((rows, LANES), lambda b: (b, 0), pipeline_mode=pl.Buffered(1)),
                  _const_spec((1, LANES)), _const_spec((1, BRANCH_WIDTH))],
        out_specs=pl.BlockSpec((rows, BRANCH_WIDTH), lambda b: (b, 0)),
        out_shape=jax.ShapeDtypeStruct((batch * seq, BRANCH_WIDTH), BF16),
        scratch_shapes=[pltpu.VMEM((rows, BRANCH_WIDTH), F32), pltpu.VMEM((rows, BRANCH_WIDTH), F32),
                        pltpu.VMEM((2 * n_seq, BRANCH_WIDTH, 2 * BRANCH_WIDTH), F32)],
        compiler_params=_cparams(("parallel",), 56),
        name="mlstm",
    )(u_ml, u_ml, u_ml, u_ml, gates, gate_b, norm_g)


CONV_PAD = 16
CONV_TILE = 64


def _conv_body(u_ref, w_ref, b_ref, g_ref, beta_ref, o_ref, pad_ref, *, seq):
    u = u_ref[...]
    a = u[:, :BRANCH_WIDTH].astype(F32)
    gate = u[:, BRANCH_WIDTH:].astype(F32)
    zeros = jnp.zeros((CONV_PAD, BRANCH_WIDTH), F32)
    pad_ref[0:CONV_PAD, :] = zeros
    pad_ref[CONV_PAD + seq:, :] = zeros
    pad_ref[CONV_PAD:CONV_PAD + seq, :] = a * _sigmoid(gate)
    first = CONV_PAD - CONV_WIDTH // 2
    span = CONV_TILE + 2 * CONV_PAD - SUBLANES

    def tile(t, carry):
        base = pl.multiple_of(t * CONV_TILE, CONV_TILE)
        halves = []
        for c in range(BRANCH_WIDTH // LANES):
            lanes = slice(c * LANES, (c + 1) * LANES)
            win = pad_ref[pl.ds(base, CONV_TILE + 2 * CONV_PAD), lanes]
            n_blk = CONV_TILE // SUBLANES
            acc = jnp.zeros((n_blk, SUBLANES, LANES), F32)
            for r in range(SUBLANES):
                shifted = win[r:r + span, :].reshape(span // SUBLANES, SUBLANES, LANES)
                for j in range(CONV_WIDTH):
                    a8, rj = divmod(first + j, SUBLANES)
                    if rj == r:
                        acc = acc + shifted[a8:a8 + n_blk] * w_ref[j, :, lanes][None]
            halves.append(acc.reshape(CONV_TILE, LANES))
        y = jnp.concatenate(halves, axis=1) + b_ref[...]
        mu = jnp.mean(y, axis=-1, keepdims=True)
        yc = y - mu
        var = jnp.mean(yc * yc, axis=-1, keepdims=True)
        z = yc * lax.rsqrt(var + NORM_EPS) * g_ref[...] + beta_ref[...]
        o_ref[pl.ds(base, CONV_TILE), :] = (z * _sigmoid(z)).astype(BF16)
        return carry

    lax.fori_loop(0, seq // CONV_TILE, tile, 0)


def _conv(u_cv, w, b, g, beta, batch, seq):
    return pl.pallas_call(
        functools.partial(_conv_body, seq=seq),
        grid=(batch,),
        in_specs=[pl.BlockSpec((seq, CONV_COLS), lambda i: (i, 0)),
                  _const_spec(w.shape), _const_spec(b.shape), _const_spec(g.shape), _const_spec(beta.shape)],
        out_specs=pl.BlockSpec((seq, BRANCH_WIDTH), lambda i: (i, 0)),
        out_shape=jax.ShapeDtypeStruct((batch * seq, BRANCH_WIDTH), BF16),
        scratch_shapes=[pltpu.VMEM((seq + 2 * CONV_PAD, BRANCH_WIDTH), F32)],
        compiler_params=_cparams(("parallel",), 32),
        name="conv",
    )(u_cv, w, b, g, beta)


ATT_Q_TILE = 256


def _gqa_body(q_ref, k_ref, v_ref, o_ref):
    k = k_ref[...]
    v = v_ref[...]
    lane = lax.broadcasted_iota(jnp.int32, (ATT_Q_TILE, LANES), 1)
    low = lane < HEAD_DIM
    for c in range(2):
        qc = q_ref[:, c * LANES:(c + 1) * LANES]
        halves = []
        for keep in (low, jnp.logical_not(low)):
            qh = jnp.where(keep, qc, jnp.zeros_like(qc))
            s = lax.dot_general(qh, k, (((1,), (1,)), ((), ())), preferred_element_type=F32)
            m = jnp.max(s, axis=1, keepdims=True)
            p = jnp.exp(s - m)
            l = jnp.sum(p, axis=1, keepdims=True)
            halves.append(jnp.dot(p.astype(BF16), v, preferred_element_type=F32) / l)
        o_ref[:, c * LANES:(c + 1) * LANES] = jnp.where(low, halves[0], halves[1]).astype(BF16)


def _gqa(qk, v, batch, seq):
    tiles = seq // ATT_Q_TILE
    return pl.pallas_call(
        _gqa_body,
        grid=(batch, tiles),
        in_specs=[pl.BlockSpec((ATT_Q_TILE, N_HEADS * HEAD_DIM), lambda b, i: (b * tiles + i, 0)),
                  pl.BlockSpec((seq, LANES), lambda b, i: (b, 2)),
                  pl.BlockSpec((seq, LANES), lambda b, i: (b, 0))],
        out_specs=pl.BlockSpec((ATT_Q_TILE, BRANCH_WIDTH), lambda b, i: (b * tiles + i, 0)),
        out_shape=jax.ShapeDtypeStruct((batch * seq, BRANCH_WIDTH), BF16),
        compiler_params=_cparams(("parallel", "parallel"), 48),
        name="gqa",
    )(qk, qk, v)


def _sigmoid(x):
    return 0.5 * jnp.tanh(0.5 * x) + 0.5


def _merge_body(x_ref, g_ref, b0_ref, b1_ref, b2_ref, b3_ref, wg_ref, bg_ref, wb_ref, wo_ref, o_ref):
    x = x_ref[...]
    h = _rms(x, g_ref[...]).astype(BF16)
    acc = jnp.zeros(x.shape, F32)
    for n, br_ref in enumerate((b0_ref, b1_ref, b2_ref, b3_ref)):
        cols = slice(n * D_MODEL, (n + 1) * D_MODEL)
        gl = jnp.dot(h, wg_ref[:, cols], preferred_element_type=F32) + bg_ref[:, cols]
        pr = jnp.dot(br_ref[...], wb_ref[n], preferred_element_type=F32)
        acc = acc + _sigmoid(gl) * pr
    o_ref[...] = x + jnp.dot(acc.astype(BF16), wo_ref[...], preferred_element_type=F32)


def _merge(x, g, branches, w_gate, b_gate, w_branch, w_out):
    n_tok = x.shape[0]
    tm = TOKEN_TILE
    row = lambda i: (i, 0)
    return pl.pallas_call(
        _merge_body,
        grid=(n_tok // tm,),
        in_specs=[pl.BlockSpec((tm, D_MODEL), row), _const_spec((1, D_MODEL))]
        + [pl.BlockSpec((tm, BRANCH_WIDTH), row)] * N_BRANCHES
        + [_const_spec(w_gate.shape), _const_spec(b_gate.shape), _const_spec(w_branch.shape), _const_spec(w_out.shape)],
        out_specs=pl.BlockSpec((tm, D_MODEL), row),
        out_shape=jax.ShapeDtypeStruct((n_tok, D_MODEL), F32),
        compiler_params=_cparams(("parallel",), 56),
        name="merge",
    )(x, g, *branches, w_gate, b_gate, w_branch, w_out)


MOE_TOKEN_TILE = 1024
MOE_CHUNK = 320
ROUTER_GROUP_LANE = ROUTER_E0 + MOE_N_EXPERTS


def _split_bf16(x):
    hi = x.astype(BF16)
    return hi, (x - hi.astype(F32)).astype(BF16)


def _router(h, wr_hi_ref, wr_lo_ref, br_ref):
    h_hi, h_lo = _split_bf16(h)
    logits = (jnp.dot(h_hi, wr_hi_ref[...], preferred_element_type=F32)
              + jnp.dot(h_lo, wr_hi_ref[...], preferred_element_type=F32)
              + jnp.dot(h_hi, wr_lo_ref[...], preferred_element_type=F32)) + br_ref[...]
    lane = lax.broadcasted_iota(jnp.int32, logits.shape, 1)
    big = jnp.int32(ROUTER_LANES)
    is_g = lane < MOE_GROUPS
    gl = jnp.where(is_g, logits, NEG_INF)
    g_max = jnp.max(gl, axis=1, keepdims=True)
    g_idx = jnp.min(jnp.where(is_g & (gl == g_max), lane, big), axis=1, keepdims=True)
    g_gate = 1.0 / jnp.sum(jnp.where(is_g, jnp.exp(gl - g_max), 0.0), axis=1, keepdims=True)
    e_lo = ROUTER_E0 + g_idx * MOE_EXPERTS_PER_GROUP
    in_grp = (lane >= e_lo) & (lane < e_lo + MOE_EXPERTS_PER_GROUP)
    el = jnp.where(in_grp, logits, NEG_INF)
    top1 = jnp.max(el, axis=1, keepdims=True)
    idx1 = jnp.min(jnp.where(in_grp & (el == top1), lane, big), axis=1, keepdims=True)
    rest = in_grp & (lane != idx1)
    el2 = jnp.where(rest, logits, NEG_INF)
    top2 = jnp.max(el2, axis=1, keepdims=True)
    idx2 = jnp.min(jnp.where(rest & (el2 == top2), lane, big), axis=1, keepdims=True)
    e2 = jnp.exp(top2 - top1)
    w1 = g_gate / (1.0 + e2)
    w2 = g_gate * e2 / (1.0 + e2)
    rec = jnp.where(lane == idx1, w1, 0.0) + jnp.where(lane == idx2, w2, 0.0)
    return jnp.where(lane == ROUTER_GROUP_LANE, g_idx.astype(F32), rec)


def _moe_body(x_ref, g_ref, wrh_ref, wrl_ref, br_ref, tri_ref, w1_ref, w3_ref, w2_ref, fg_ref, o_ref, h_ref, acc_ref, *,
              final):
    tm = MOE_TOKEN_TILE
    x = x_ref[...]
    h = _rms(x, g_ref[...])
    h_ref[...] = h.astype(BF16)
    rec = _router(h, wrh_ref, wrl_ref, br_ref)
    rec_hi, rec_lo = _split_bf16(rec)
    lane = lax.broadcasted_iota(jnp.int32, rec.shape, 1).astype(F32)
    gid = rec[:, ROUTER_GROUP_LANE:ROUTER_GROUP_LANE + 1]
    own = lane == gid
    before = jnp.dot(tri_ref[...], jnp.where(own, 1.0, 0.0).astype(BF16), preferred_element_type=F32)
    rank = jnp.sum(jnp.where(own, before, 0.0), axis=1, keepdims=True)
    totals = jnp.sum(jnp.where(own, 1.0, 0.0), axis=0, keepdims=True)
    acc_ref[...] = jnp.zeros_like(acc_ref)
    slot = lax.broadcasted_iota(jnp.int32, (tm, MOE_CHUNK), 1).astype(F32)
    contract0 = (((0,), (0,)), ((), ()))
    for grp in range(MOE_GROUPS):
        n_grp = jnp.sum(jnp.where(lane[0:1, :] == grp, totals, 0.0)).astype(jnp.int32)
        rank_grp = jnp.where(gid == grp, rank, -1.0)
        for k in range(-(-tm // MOE_CHUNK)):

            @pl.when(n_grp > k * MOE_CHUNK)
            def _():
                sel = jnp.where(rank_grp - (k * MOE_CHUNK) == slot, 1.0, 0.0).astype(BF16)
                xc = lax.dot_general(sel, h_ref[...], contract0, preferred_element_type=F32).astype(BF16)
                cw = (lax.dot_general(sel, rec_hi, contract0, preferred_element_type=F32)
                      + lax.dot_general(sel, rec_lo, contract0, preferred_element_type=F32))
                yc = jnp.zeros((MOE_CHUNK, D_MODEL), F32)
                for e in range(MOE_EXPERTS_PER_GROUP):
                    ge = grp * MOE_EXPERTS_PER_GROUP + e
                    a = jnp.dot(xc, w1_ref[ge], preferred_element_type=F32)
                    b = jnp.dot(xc, w3_ref[ge], preferred_element_type=F32)
                    c = cw[:, ROUTER_E0 + ge:ROUTER_E0 + ge + 1]
                    hid = (a * _sigmoid(a) * b * c).astype(BF16)
                    yc = yc + jnp.dot(hid, w2_ref[ge], preferred_element_type=F32)
                acc_ref[...] += jnp.dot(sel, yc.astype(BF16), preferred_element_type=F32)

    y = x + acc_ref[...]
    o_ref[...] = _rms(y, fg_ref[...]) if final else y


def _moe(x, g, wr_hi, wr_lo, br, w1, w3, w2, final_g, final):
    n_tok = x.shape[0]
    tm = MOE_TOKEN_TILE
    row = lambda i: (i, 0)
    once = lambda shape: pl.BlockSpec(shape, lambda i: (0,) * len(shape), pipeline_mode=pl.Buffered(1))
    earlier = np.tril(np.ones((tm, tm), np.float32), -1)
    return pl.pallas_call(
        functools.partial(_moe_body, final=final),
        grid=(n_tok // tm,),
        in_specs=[pl.BlockSpec((tm, D_MODEL), row), _const_spec((1, D_MODEL)),
                  _const_spec(wr_hi.shape), _const_spec(wr_lo.shape), _const_spec(br.shape), once((tm, tm)),
                  once(w1.shape), once(w3.shape), once(w2.shape), _const_spec((1, D_MODEL))],
        out_specs=pl.BlockSpec((tm, D_MODEL), row),
        out_shape=jax.ShapeDtypeStruct((n_tok, D_MODEL), F32),
        scratch_shapes=[pltpu.VMEM((tm, D_MODEL), BF16), pltpu.VMEM((tm, D_MODEL), F32)],
        compiler_params=_cparams(("parallel",), 60),
        name="moe",
    )(x, g, wr_hi, wr_lo, br, jnp.asarray(earlier, BF16), w1, w3, w2, final_g)


def _rope_tables(seq):
    t = jnp.arange(seq)
    pos = jnp.stack([t // GRID_W, t % GRID_W], axis=-1).astype(F32)
    inv = ROPE_THETA ** (-jnp.arange(0, ROPE_AXIS_DIM, 2, dtype=F32) / ROPE_AXIS_DIM)
    ang = pos[..., None] * inv
    ang = jnp.concatenate([ang, ang], axis=-1).reshape(seq, HEAD_DIM)
    first = (jnp.arange(HEAD_DIM) % ROPE_AXIS_DIM) < (ROPE_AXIS_DIM // 2)
    cos = jnp.cos(ang)
    sin = jnp.where(first[None, :], -jnp.sin(ang), jnp.sin(ang))
    reps = QK_COLS // HEAD_DIM
    return jnp.tile(cos, (1, reps)), jnp.tile(sin, (1, reps))


ATT_HEAD_ORDER = (0, 2, 1, 3)


def _pack_layer(l, w_in, na_rpb, ml_gate_b, ml_norm_g, conv_dw_w, conv_dw_b, conv_norm_g, conv_norm_b,
                att_q_norm_g, att_k_norm_g, w_branch, w_gate, b_gate, w_out, moe_w_group, moe_b_group,
                moe_w_expert, moe_b_expert, moe_w1, moe_w3, moe_w2):
    scale = HEAD_DIM ** -0.5
    w = w_in[l]
    o0 = 0
    wna = w[:, o0:o0 + NA_COLS]
    wna = jnp.concatenate([wna[:, :BRANCH_WIDTH] * scale, wna[:, BRANCH_WIDTH:]], axis=1)
    o0 += NA_COLS
    wml = w[:, o0:o0 + 4 * BRANCH_WIDTH]
    wml = jnp.concatenate([wml[:, :BRANCH_WIDTH], wml[:, BRANCH_WIDTH:2 * BRANCH_WIDTH] * scale,
                           wml[:, 2 * BRANCH_WIDTH:]], axis=1)
    o0 += 4 * BRANCH_WIDTH
    n_gates = ML_N_GATES * N_HEADS
    wg = jnp.pad(w[:, o0:o0 + n_gates], ((0, 0), (0, LANES - n_gates)))
    o0 += n_gates
    wcv = w[:, o0:o0 + CONV_COLS]
    o0 += CONV_COLS
    wat = w[:, o0:o0 + ATT_COLS]
    wq = wat[:, :N_HEADS * HEAD_DIM].reshape(D_MODEL, N_HEADS, HEAD_DIM)[:, ATT_HEAD_ORDER, :]
    wat = jnp.concatenate([wq.reshape(D_MODEL, N_HEADS * HEAD_DIM), wat[:, N_HEADS * HEAD_DIM:]], axis=1)
    qkg = jnp.concatenate([jnp.tile(att_q_norm_g[l] * scale, N_HEADS), jnp.tile(att_k_norm_g[l], ATT_KV_HEADS)])
    wb = w_branch[l]
    wb3 = wb[3].reshape(N_HEADS, HEAD_DIM, D_MODEL)[ATT_HEAD_ORDER, :, :].reshape(BRANCH_WIDTH, D_MODEL)
    wb = jnp.concatenate([wb[:3], wb3[None]], axis=0)
    wr = jnp.concatenate([moe_w_group[l], moe_w_expert[l].reshape(D_MODEL, MOE_N_EXPERTS)], axis=1)
    n_r = MOE_GROUPS + MOE_N_EXPERTS
    br = jnp.concatenate([moe_b_group[l], moe_b_expert[l].reshape(MOE_N_EXPERTS)])
    wr_hi, wr_lo = _split_bf16(jnp.pad(wr, ((0, 0), (0, ROUTER_LANES - n_r))))
    return dict(
        wna=wna.astype(BF16), wml=wml.astype(BF16), wg=wg.astype(BF16), wcv=wcv.astype(BF16), wat=wat.astype(BF16),
        qkg=qkg.reshape(1, QK_COLS).astype(F32),
        na_bias=_na_bias_table(na_rpb[l]),
        gate_b=jnp.pad(ml_gate_b[l].reshape(1, n_gates), ((0, 0), (0, LANES - n_gates))).astype(F32),
        ml_norm_g=ml_norm_g[l].reshape(1, BRANCH_WIDTH),
        conv_w=jnp.broadcast_to(conv_dw_w[l][:, None, :], (CONV_WIDTH, SUBLANES, BRANCH_WIDTH)), conv_b=conv_dw_b[l].reshape(1, BRANCH_WIDTH),
        conv_g=conv_norm_g[l].reshape(1, BRANCH_WIDTH), conv_beta=conv_norm_b[l].reshape(1, BRANCH_WIDTH),
        w_gate=w_gate[l].astype(BF16), b_gate=b_gate[l].reshape(1, N_BRANCHES * D_MODEL),
        w_branch=wb.astype(BF16), w_out=w_out[l].astype(BF16),
        wr_hi=wr_hi, wr_lo=wr_lo, br=jnp.pad(br, (0, ROUTER_LANES - n_r)).reshape(1, ROUTER_LANES),
        w1=moe_w1[l].astype(BF16), w3=moe_w3[l].astype(BF16), w2=moe_w2[l].astype(BF16),
    )


def _trunk(x, layers, norm_mix_g, norm_ffn_g, final_norm_g, ones_qk, depth):
    batch, seq, _ = x.shape
    xf = x.reshape(batch * seq, D_MODEL)
    cos_t, sin_t = _rope_tables(seq)
    fg = final_norm_g.reshape(1, D_MODEL)
    for l in range(depth):
        p = layers[l]
        gm = norm_mix_g[l].reshape(1, D_MODEL)
        u_na, u_ml, gates, u_cv, qk, v = _inproj(xf, gm, p["wna"], p["wml"], p["wg"], p["wcv"], p["wat"], ones_qk,
                                                 p["qkg"], cos_t, sin_t, seq)
        o_na = _na(u_na, p["na_bias"], batch, seq)
        o_ml = _mlstm(u_ml, gates, p["gate_b"], p["ml_norm_g"], batch, seq)
        o_cv = _conv(u_cv, p["conv_w"], p["conv_b"], p["conv_g"], p["conv_beta"], batch, seq)
        o_at = _gqa(qk, v, batch, seq)
        xf = _merge(xf, gm, (o_na, o_ml, o_cv, o_at), p["w_gate"], p["b_gate"], p["w_branch"], p["w_out"])
        xf = _moe(xf, norm_ffn_g[l].reshape(1, D_MODEL), p["wr_hi"], p["wr_lo"], p["br"], p["w1"], p["w3"], p["w2"], fg,
                  final=(l == depth - 1))
    return xf.reshape(batch, seq, D_MODEL)


def kernel(x_prompt, x_sample, norm_mix_g, w_in, na_rpb, ml_gate_b, ml_norm_g, conv_dw_w, conv_dw_b, conv_norm_g, conv_norm_b, att_q_norm_g, att_k_norm_g, w_branch, w_gate, b_gate, w_out, norm_ffn_g, moe_w_group, moe_b_group, moe_w_expert, moe_b_expert, moe_w1, moe_w3, moe_w2, final_norm_g):
    depth = w_in.shape[0]
    layers = [_pack_layer(l, w_in, na_rpb, ml_gate_b, ml_norm_g, conv_dw_w, conv_dw_b, conv_norm_g, conv_norm_b,
                          att_q_norm_g, att_k_norm_g, w_branch, w_gate, b_gate, w_out, moe_w_group, moe_b_group,
                          moe_w_expert, moe_b_expert, moe_w1, moe_w3, moe_w2) for l in range(depth)]
    hid = np.arange(QK_COLS) // HEAD_DIM
    ones_qk = jnp.asarray(hid[:, None] == hid[None, :], BF16)
    y_prompt = _trunk(x_prompt, layers, norm_mix_g, norm_ffn_g, final_norm_g, ones_qk, depth)
    y_sample = _trunk(x_sample, layers, norm_mix_g, norm_ffn_g, final_norm_g, ones_qk, depth)
    return (y_prompt, y_sample)
```

```python
import functools

import jax
import jax.numpy as jnp
import numpy as np
from jax import lax
from jax.experimental import pallas as pl
from jax.experimental.pallas import tpu as pltpu

F32 = jnp.float32
BF16 = jnp.bfloat16

D_MODEL = 1024
GRID_W = 64
HEAD_DIM = 64
BRANCH_WIDTH = 256
N_BRANCHES = 4
N_HEADS = 4
NA_KH = 8
NA_KW = 16
ML_CHUNK = 64
ML_N_GATES = 4
CONV_WIDTH = 31
ATT_KV_HEADS = 2
ROPE_THETA = 10000.0
ROPE_AXIS_DIM = HEAD_DIM // 2
MOE_GROUPS = 4
MOE_EXPERTS_PER_GROUP = 4
MOE_N_EXPERTS = 16
MOE_D_EXPERT = 256
NORM_EPS = 1e-6
NEG_INF = -1e30

NA_COLS = 3 * BRANCH_WIDTH
ML_COLS = 4 * BRANCH_WIDTH + ML_N_GATES * N_HEADS
CONV_COLS = 2 * BRANCH_WIDTH
ATT_COLS = (N_HEADS + 2 * ATT_KV_HEADS) * HEAD_DIM
QK_COLS = (N_HEADS + ATT_KV_HEADS) * HEAD_DIM

LANES = 128
SUBLANES = 8
ROUTER_LANES = LANES
ROUTER_E0 = MOE_GROUPS
V7X_VMEM_BYTES = 64 * 1024 * 1024

TOKEN_TILE = 512


def _cparams(semantics, vmem_mb):
    assert vmem_mb * 1024 * 1024 < V7X_VMEM_BYTES
    return pltpu.CompilerParams(dimension_semantics=semantics, vmem_limit_bytes=vmem_mb * 1024 * 1024)


def _const_spec(shape):
    nd = len(shape)
    return pl.BlockSpec(shape, lambda *_: (0,) * nd)


def _rms(x, g):
    ms = jnp.mean(x * x, axis=-1, keepdims=True)
    return x * lax.rsqrt(ms + NORM_EPS) * g


def _inproj_body(x_ref, g_ref, wna_ref, wml_ref, wg_ref, wcv_ref, wat_ref, ones_ref, qkg_ref, cos_ref, sin_ref,
                 una_ref, uml_ref, gates_ref, ucv_ref, qk_ref, v_ref):
    h = _rms(x_ref[...], g_ref[...]).astype(BF16)
    una_ref[...] = jnp.dot(h, wna_ref[...], preferred_element_type=F32).astype(BF16)
    uml_ref[...] = jnp.dot(h, wml_ref[...], preferred_element_type=F32).astype(BF16)
    gates_ref[...] = jnp.dot(h, wg_ref[...], preferred_element_type=F32)
    ucv_ref[...] = jnp.dot(h, wcv_ref[...], preferred_element_type=F32).astype(BF16)
    ua = jnp.dot(h, wat_ref[...], preferred_element_type=F32)
    qk = ua[:, :QK_COLS]
    ms = jnp.dot((qk * qk).astype(BF16), ones_ref[...], preferred_element_type=F32) * (1.0 / HEAD_DIM)
    qn = qk * lax.rsqrt(ms + NORM_EPS) * qkg_ref[...]
    half = ROPE_AXIS_DIM // 2
    rot = []
    for c in range(QK_COLS // LANES):
        xc = qn[:, c * LANES:(c + 1) * LANES]
        lane = lax.broadcasted_iota(jnp.int32, xc.shape, 1)
        rot.append(jnp.where((lane % ROPE_AXIS_DIM) < half, pltpu.roll(xc, LANES - half, 1), pltpu.roll(xc, half, 1)))
    rot = jnp.concatenate(rot, axis=1)
    qk_ref[...] = (qn * cos_ref[...] + rot * sin_ref[...]).astype(BF16)
    v_ref[...] = ua[:, QK_COLS:].astype(BF16)


def _inproj(x, g, wna, wml, wg, wcv, wat, ones_qk, qkg, cos_t, sin_t, seq):
    n_tok = x.shape[0]
    tm = TOKEN_TILE
    tiles_per_seq = seq // tm
    row = lambda i: (i, 0)
    pos = lambda i: (i % tiles_per_seq, 0)
    outs = [
        jax.ShapeDtypeStruct((n_tok, NA_COLS), BF16),
        jax.ShapeDtypeStruct((n_tok, 4 * BRANCH_WIDTH), BF16),
        jax.ShapeDtypeStruct((n_tok, LANES), F32),
        jax.ShapeDtypeStruct((n_tok, CONV_COLS), BF16),
        jax.ShapeDtypeStruct((n_tok, QK_COLS), BF16),
        jax.ShapeDtypeStruct((n_tok, ATT_KV_HEADS * HEAD_DIM), BF16),
    ]
    return pl.pallas_call(
        _inproj_body,
        grid=(n_tok // tm,),
        in_specs=[
            pl.BlockSpec((tm, D_MODEL), row),
            _const_spec((1, D_MODEL)),
            _const_spec(wna.shape), _const_spec(wml.shape), _const_spec(wg.shape), _const_spec(wcv.shape),
            _const_spec(wat.shape), _const_spec(ones_qk.shape), _const_spec(qkg.shape),
            pl.BlockSpec((tm, QK_COLS), pos), pl.BlockSpec((tm, QK_COLS), pos),
        ],
        out_specs=[pl.BlockSpec((tm, o.shape[1]), row) for o in outs],
        out_shape=outs,
        compiler_params=_cparams(("parallel",), 48),
        name="inproj",
    )(x, g, wna, wml, wg, wcv, wat, ones_qk, qkg, cos_t, sin_t)


def _block_diag_mask(rows, cols):
    r = lax.broadcasted_iota(jnp.int32, (rows, cols), 0) // HEAD_DIM
    c = lax.broadcasted_iota(jnp.int32, (rows, cols), 1) // HEAD_DIM
    return r == (c % N_HEADS)


def _tile_heads(x):
    t = jnp.concatenate([x] * N_HEADS, axis=0)
    return jnp.where(_block_diag_mask(N_HEADS * HEAD_DIM, BRANCH_WIDTH), t, jnp.zeros_like(t))


def _na_body(q_ref, k_ref, v_ref, bias_ref, o_ref, *, rows):
    n_keys = NA_KH * GRID_W

    def one_row(r, carry):
        r0 = jnp.clip(r - NA_KH // 2, 0, rows - NA_KH)
        delta = r0 - r + NA_KH - 1
        q = q_ref[pl.ds(pl.multiple_of(r * GRID_W, GRID_W), GRID_W), :]
        kk = k_ref[pl.ds(pl.multiple_of(r0 * GRID_W, GRID_W), n_keys), :]
        vv = v_ref[pl.ds(pl.multiple_of(r0 * GRID_W, GRID_W), n_keys), :]
        s = lax.dot_general(_tile_heads(q), kk, (((1,), (1,)), ((), ())), preferred_element_type=F32)
        s = s + bias_ref[delta]
        m = jnp.max(s, axis=1, keepdims=True)
        p = jnp.exp(s - m)
        l = jnp.sum(p, axis=1, keepdims=True)
        of = jnp.dot(p.astype(BF16), vv, preferred_element_type=F32) / l
        of = jnp.where(_block_diag_mask(N_HEADS * HEAD_DIM, BRANCH_WIDTH), of, 0.0)
        o = of[0:HEAD_DIM] + of[HEAD_DIM:2 * HEAD_DIM] + of[2 * HEAD_DIM:3 * HEAD_DIM] + of[3 * HEAD_DIM:]
        o_ref[pl.ds(pl.multiple_of(r * GRID_W, GRID_W), GRID_W), :] = o.astype(BF16)
        return carry

    lax.fori_loop(0, rows, one_row, 0)


def _na_bias_table(rpb):
    qc = np.arange(GRID_W)[:, None]
    kc = np.arange(GRID_W)[None, :]
    win = np.clip(qc - NA_KW // 2, 0, GRID_W - NA_KW)
    in_win = (kc >= win) & (kc < win + NA_KW)
    col_off = np.clip(kc - qc + NA_KW - 1, 0, 2 * NA_KW - 2)
    n_ro, n_co = 2 * NA_KH - 1, 2 * NA_KW - 1
    pick = np.zeros((n_co, GRID_W * GRID_W), np.float32)
    pick[col_off.reshape(-1), np.arange(GRID_W * GRID_W)] = 1.0
    c = jnp.dot(rpb.astype(F32).reshape(N_HEADS * n_ro, n_co), pick, precision=lax.Precision.HIGHEST)
    c = jnp.where(in_win.reshape(1, -1), c, NEG_INF).reshape(N_HEADS, n_ro, GRID_W, GRID_W)
    c = c.transpose(0, 2, 1, 3)
    b = jnp.stack([c[:, :, d:d + NA_KH, :] for d in range(NA_KH)], axis=0)
    return b.reshape(NA_KH, N_HEADS * GRID_W, NA_KH * GRID_W)


def _na(u_na, bias, batch, seq):
    rows = seq // GRID_W
    return pl.pallas_call(
        functools.partial(_na_body, rows=rows),
        grid=(batch,),
        in_specs=[
            pl.BlockSpec((seq, BRANCH_WIDTH), lambda b: (b, 0)),
            pl.BlockSpec((seq, BRANCH_WIDTH), lambda b: (b, 1)),
            pl.BlockSpec((seq, BRANCH_WIDTH), lambda b: (b, 2)),
            _const_spec(bias.shape),
        ],
        out_specs=pl.BlockSpec((seq, BRANCH_WIDTH), lambda b: (b, 0)),
        out_shape=jax.ShapeDtypeStruct((batch * seq, BRANCH_WIDTH), BF16),
        compiler_params=_cparams(("parallel",), 48),
        name="na",
    )(u_na, u_na, u_na, bias)


def _expand_heads(cols):
    hid = lax.broadcasted_iota(jnp.int32, (ML_CHUNK, BRANCH_WIDTH), 1) // HEAD_DIM
    out = jnp.broadcast_to(cols[:, 0:1], (ML_CHUNK, BRANCH_WIDTH))
    for h in range(1, N_HEADS):
        out = jnp.where(hid == h, cols[:, h:h + 1], out)
    return out


def _head_rowmax(x):
    hid = lax.broadcasted_iota(jnp.int32, x.shape, 1) // HEAD_DIM
    out = jnp.zeros_like(x)
    for h in range(N_HEADS):
        mh = jnp.max(jnp.where(hid == h, x, NEG_INF), axis=1, keepdims=True)
        out = jnp.where(hid == h, mh, out)
    return out


def _mlstm_step_fn(reverse, row0, q_ref, k_ref, v_ref, gates_ref, gb_ref, h_ref, state_ref, n_chunks):
    L = ML_CHUNK
    W = BRANCH_WIDTH
    sub = lax.broadcasted_iota(jnp.int32, (L, W), 0)
    pos = lax.broadcasted_iota(jnp.int32, (L, W), 1) % HEAD_DIM
    diag = sub == pos
    causal = (pos >= sub) if reverse else (pos <= sub)
    tr = lax.broadcasted_iota(jnp.int32, (L, L), 0)
    tc = lax.broadcasted_iota(jnp.int32, (L, L), 1)
    tri = ((tc >= tr) if reverse else (tc <= tr)).astype(F32)
    ones_bd = _block_diag_mask(W, W).astype(BF16)
    bd2 = _block_diag_mask(W, 2 * W)
    i_off, f_off = (2 * N_HEADS, 3 * N_HEADS) if reverse else (0, N_HEADS)
    state_ref[...] = jnp.zeros_like(state_ref)

    def step(ci, m_run):
        c = (n_chunks - 1 - ci) if reverse else ci
        sl = pl.ds(pl.multiple_of(row0 + c * L, L), L)
        q = q_ref[sl, :]
        k = k_ref[sl, :]
        v = v_ref[sl, :]
        g = gates_ref[sl, :] + gb_ref[...]
        i_e = _expand_heads(g[:, i_off:i_off + N_HEADS])
        f_e = _expand_heads(jax.nn.log_sigmoid(g[:, f_off:f_off + N_HEADS]))
        b = jnp.dot(tri, f_e, precision=lax.Precision.HIGHEST, preferred_element_type=F32)
        b_end = b[0:1, :] if reverse else b[L - 1:L, :]
        rrow = jnp.sum(jnp.where(diag, i_e - b, 0.0), axis=0, keepdims=True)
        log_d = jnp.where(causal, b + rrow, NEG_INF)
        inter = b + m_run
        m_row = jnp.maximum(inter, _head_rowmax(log_d))
        d = jnp.exp(log_d - m_row)
        w_inter = jnp.exp(inter - m_row)
        s = lax.dot_general(q, _tile_heads(k), (((1,), (1,)), ((), ())), preferred_element_type=F32)
        p = (s * d).astype(BF16)
        r_loc = jnp.dot(p, jnp.concatenate([_tile_heads(v), ones_bd], axis=1), preferred_element_type=F32)
        r_int = jnp.dot(q, state_ref[...].astype(BF16), preferred_element_type=F32)
        num = r_loc[:, :W] + w_inter * r_int[:, :W]
        den = r_loc[:, W:] + w_inter * r_int[:, W:]
        h_ref[sl, :] = num / jnp.maximum(jnp.abs(den), jnp.exp(-m_row))
        log_w = b_end - b + i_e
        m_new = jnp.maximum(b_end + m_run, jnp.max(log_w, axis=0, keepdims=True))
        w = jnp.exp(log_w - m_new)
        decay = jnp.exp(b_end + m_run - m_new)
        upd_rhs = jnp.concatenate([(w * v.astype(F32)).astype(BF16), w.astype(BF16)], axis=1)
        upd = lax.dot_general(k, upd_rhs, (((0,), (0,)), ((), ())), preferred_element_type=F32)
        upd = jnp.where(bd2, upd, 0.0)
        state_ref[...] = jnp.concatenate([decay, decay], axis=1) * state_ref[...] + upd
        return m_new

    return step


ML_ROWS_PER_STEP = 8192


def _mlstm_body(q_ref, k_ref, v_ref, og_ref, gates_ref, gb_ref, ng_ref, o_ref, hf_ref, hb_ref, state_ref, *, seq, n_seq):
    n_chunks = seq // ML_CHUNK
    steps = []
    for j in range(n_seq):
        for reverse in (False, True):
            steps.append(_mlstm_step_fn(reverse, j * seq, q_ref, k_ref, v_ref, gates_ref, gb_ref,
                                        hb_ref if reverse else hf_ref, state_ref.at[2 * j + int(reverse)], n_chunks))
    m0 = jnp.full((1, BRANCH_WIDTH), NEG_INF, F32)
    lax.fori_loop(0, n_chunks, lambda ci, ms: tuple(step(ci, m) for step, m in zip(steps, ms)), (m0,) * len(steps))
    ones_bd = _block_diag_mask(BRANCH_WIDTH, BRANCH_WIDTH).astype(BF16)
    tile = 256

    def finish(t, carry):
        sl = pl.ds(pl.multiple_of(t * tile, tile), tile)
        hs = hf_ref[sl, :] + hb_ref[sl, :]
        ms = jnp.dot((hs * hs).astype(BF16), ones_bd, preferred_element_type=F32) * (1.0 / HEAD_DIM)
        hm = hs * lax.rsqrt(ms + NORM_EPS) * ng_ref[...]
        o_ref[sl, :] = (hm * _sigmoid(og_ref[sl, :].astype(F32))).astype(BF16)
        return carry

    lax.fori_loop(0, n_seq * seq // tile, finish, 0)


def _mlstm(u_ml, gates, gate_b, norm_g, batch, seq):
    n_seq = min(batch, ML_ROWS_PER_STEP // seq)
    rows = n_seq * seq
    col = lambda j: pl.BlockSpec((rows, BRANCH_WIDTH), lambda b: (b, j), pipeline_mode=pl.Buffered(1))
    return pl.pallas_call(
        functools.partial(_mlstm_body, seq=seq, n_seq=n_seq),
        grid=(batch // n_seq,),
        in_specs=[col(0), col(1), col(2), col(3),
                  pl.BlockSpec((rows, LANES), lambda b: (b, 0), pipeline_mode=pl.Buffered(1)),
                  _const_spec((1, LANES)), _const_spec((1, BRANCH_WIDTH))],
        out_specs=pl.BlockSpec((rows, BRANCH_WIDTH), lambda b: (b, 0)),
        out_shape=jax.ShapeDtypeStruct((batch * seq, BRANCH_WIDTH), BF16),
        scratch_shapes=[pltpu.VMEM((rows, BRANCH_WIDTH), F32), pltpu.VMEM((rows, BRANCH_WIDTH), F32),
                        pltpu.VMEM((2 * n_seq, BRANCH_WIDTH, 2 * BRANCH_WIDTH), F32)],
        compiler_params=_cparams(("parallel",), 56),
        name="mlstm",
    )(u_ml, u_ml, u_ml, u_ml, gates, gate_b, norm_g)


CONV_PAD = 16
CONV_TILE = 64


def _conv_body(u_ref, w_ref, b_ref, g_ref, beta_ref, o_ref, pad_ref, *, seq):
    u = u_ref[...]
    a = u[:, :BRANCH_WIDTH].astype(F32)
    gate = u[:, BRANCH_WIDTH:].astype(F32)
    zeros = jnp.zeros((CONV_PAD, BRANCH_WIDTH), F32)
    pad_ref[0:CONV_PAD, :] = zeros
    pad_ref[CONV_PAD + seq:, :] = zeros
    pad_ref[CONV_PAD:CONV_PAD + seq, :] = a * _sigmoid(gate)
    first = CONV_PAD - CONV_WIDTH // 2
    span = CONV_TILE + 2 * CONV_PAD - SUBLANES

    def tile(t, carry):
        base = pl.multiple_of(t * CONV_TILE, CONV_TILE)
        halves = []
        for c in range(BRANCH_WIDTH // LANES):
            lanes = slice(c * LANES, (c + 1) * LANES)
            win = pad_ref[pl.ds(base, CONV_TILE + 2 * CONV_PAD), lanes]
            n_blk = CONV_TILE // SUBLANES
            acc = jnp.zeros((n_blk, SUBLANES, LANES), F32)
            for r in range(SUBLANES):
                shifted = win[r:r + span, :].reshape(span // SUBLANES, SUBLANES, LANES)
                for j in range(CONV_WIDTH):
                    a8, rj = divmod(first + j, SUBLANES)
                    if rj == r:
                        acc = acc + shifted[a8:a8 + n_blk] * w_ref[j, :, lanes][None]
            halves.append(acc.reshape(CONV_TILE, LANES))
        y = jnp.concatenate(halves, axis=1) + b_ref[...]
        mu = jnp.mean(y, axis=-1, keepdims=True)
        yc = y - mu
        var = jnp.mean(yc * yc, axis=-1, keepdims=True)
        z = yc * lax.rsqrt(var + NORM_EPS) * g_ref[...] + beta_ref[...]
        o_ref[pl.ds(base, CONV_TILE), :] = (z * _sigmoid(z)).astype(BF16)
        return carry

    lax.fori_loop(0, seq // CONV_TILE, tile, 0)


def _conv(u_cv, w, b, g, beta, batch, seq):
    return pl.pallas_call(
        functools.partial(_conv_body, seq=seq),
        grid=(batch,),
        in_specs=[pl.BlockSpec((seq, CONV_COLS), lambda i: (i, 0)),
                  _const_spec(w.shape), _const_spec(b.shape), _const_spec(g.shape), _const_spec(beta.shape)],
        out_specs=pl.BlockSpec((seq, BRANCH_WIDTH), lambda i: (i, 0)),
        out_shape=jax.ShapeDtypeStruct((batch * seq, BRANCH_WIDTH), BF16),
        scratch_shapes=[pltpu.VMEM((seq + 2 * CONV_PAD, BRANCH_WIDTH), F32)],
        compiler_params=_cparams(("parallel",), 32),
        name="conv",
    )(u_cv, w, b, g, beta)


ATT_Q_TILE = 256


def _gqa_body(q_ref, k_ref, v_ref, o_ref):
    k = k_ref[...]
    v = v_ref[...]
    lane = lax.broadcasted_iota(jnp.int32, (ATT_Q_TILE, LANES), 1)
    low = lane < HEAD_DIM
    for c in range(2):
        qc = q_ref[:, c * LANES:(c + 1) * LANES]
        halves = []
        for keep in (low, jnp.logical_not(low)):
            qh = jnp.where(keep, qc, jnp.zeros_like(qc))
            s = lax.dot_general(qh, k, (((1,), (1,)), ((), ())), preferred_element_type=F32)
            m = jnp.max(s, axis=1, keepdims=True)
            p = jnp.exp(s - m)
            l = jnp.sum(p, axis=1, keepdims=True)
            halves.append(jnp.dot(p.astype(BF16), v, preferred_element_type=F32) / l)
        o_ref[:, c * LANES:(c + 1) * LANES] = jnp.where(low, halves[0], halves[1]).astype(BF16)


def _gqa(qk, v, batch, seq):
    tiles = seq // ATT_Q_TILE
    return pl.pallas_call(
        _gqa_body,
        grid=(batch, tiles),
        in_specs=[pl.BlockSpec((ATT_Q_TILE, N_HEADS * HEAD_DIM), lambda b, i: (b * tiles + i, 0)),
                  pl.BlockSpec((seq, LANES), lambda b, i: (b, 2)),
                  pl.BlockSpec((seq, LANES), lambda b, i: (b, 0))],
        out_specs=pl.BlockSpec((ATT_Q_TILE, BRANCH_WIDTH), lambda b, i: (b * tiles + i, 0)),
        out_shape=jax.ShapeDtypeStruct((batch * seq, BRANCH_WIDTH), BF16),
        compiler_params=_cparams(("parallel", "parallel"), 48),
        name="gqa",
    )(qk, qk, v)


def _sigmoid(x):
    return 0.5 * jnp.tanh(0.5 * x) + 0.5


def _merge_body(x_ref, g_ref, b0_ref, b1_ref, b2_ref, b3_ref, wg_ref, bg_ref, wb_ref, wo_ref, o_ref):
    x = x_ref[...]
    h = _rms(x, g_ref[...]).astype(BF16)
    acc = jnp.zeros(x.shape, F32)
    for n, br_ref in enumerate((b0_ref, b1_ref, b2_ref, b3_ref)):
        cols = slice(n * D_MODEL, (n + 1) * D_MODEL)
        gl = jnp.dot(h, wg_ref[:, cols], preferred_element_type=F32) + bg_ref[:, cols]
        pr = jnp.dot(br_ref[...], wb_ref[n], preferred_element_type=F32)
        acc = acc + _sigmoid(gl) * pr
    o_ref[...] = x + jnp.dot(acc.astype(BF16), wo_ref[...], preferred_element_type=F32)


def _merge(x, g, branches, w_gate, b_gate, w_branch, w_out):
    n_tok = x.shape[0]
    tm = TOKEN_TILE
    row = lambda i: (i, 0)
    return pl.pallas_call(
        _merge_body,
        grid=(n_tok // tm,),
        in_specs=[pl.BlockSpec((tm, D_MODEL), row), _const_spec((1, D_MODEL))]
        + [pl.BlockSpec((tm, BRANCH_WIDTH), row)] * N_BRANCHES
        + [_const_spec(w_gate.shape), _const_spec(b_gate.shape), _const_spec(w_branch.shape), _const_spec(w_out.shape)],
        out_specs=pl.BlockSpec((tm, D_MODEL), row),
        out_shape=jax.ShapeDtypeStruct((n_tok, D_MODEL), F32),
        compiler_params=_cparams(("parallel",), 56),
        name="merge",
    )(x, g, *branches, w_gate, b_gate, w_branch, w_out)


MOE_TOKEN_TILE = 1024
MOE_CHUNK = 320
ROUTER_GROUP_LANE = ROUTER_E0 + MOE_N_EXPERTS


def _split_bf16(x):
    hi = x.astype(BF16)
    return hi, (x - hi.astype(F32)).astype(BF16)


def _router(h, wr_hi_ref, wr_lo_ref, br_ref):
    h_hi, h_lo = _split_bf16(h)
    logits = (jnp.dot(h_hi, wr_hi_ref[...], preferred_element_type=F32)
              + jnp.dot(h_lo, wr_hi_ref[...], preferred_element_type=F32)
              + jnp.dot(h_hi, wr_lo_ref[...], preferred_element_type=F32)) + br_ref[...]
    lane = lax.broadcasted_iota(jnp.int32, logits.shape, 1)
    big = jnp.int32(ROUTER_LANES)
    is_g = lane < MOE_GROUPS
    gl = jnp.where(is_g, logits, NEG_INF)
    g_max = jnp.max(gl, axis=1, keepdims=True)
    g_idx = jnp.min(jnp.where(is_g & (gl == g_max), lane, big), axis=1, keepdims=True)
    g_gate = 1.0 / jnp.sum(jnp.where(is_g, jnp.exp(gl - g_max), 0.0), axis=1, keepdims=True)
    e_lo = ROUTER_E0 + g_idx * MOE_EXPERTS_PER_GROUP
    in_grp = (lane >= e_lo) & (lane < e_lo + MOE_EXPERTS_PER_GROUP)
    el = jnp.where(in_grp, logits, NEG_INF)
    top1 = jnp.max(el, axis=1, keepdims=True)
    idx1 = jnp.min(jnp.where(in_grp & (el == top1), lane, big), axis=1, keepdims=True)
    rest = in_grp & (lane != idx1)
    el2 = jnp.where(rest, logits, NEG_INF)
    top2 = jnp.max(el2, axis=1, keepdims=True)
    idx2 = jnp.min(jnp.where(rest & (el2 == top2), lane, big), axis=1, keepdims=True)
    e2 = jnp.exp(top2 - top1)
    w1 = g_gate / (1.0 + e2)
    w2 = g_gate * e2 / (1.0 + e2)
    rec = jnp.where(lane == idx1, w1, 0.0) + jnp.where(lane == idx2, w2, 0.0)
    return jnp.where(lane == ROUTER_GROUP_LANE, g_idx.astype(F32), rec)


def _moe_body(x_ref, g_ref, wrh_ref, wrl_ref, br_ref, tri_ref, w1_ref, w3_ref, w2_ref, fg_ref, o_ref,
              h_ref, acc_ref, rh_ref, rl_ref, rank_ref, cnt_ref, *, final):
    tm = MOE_TOKEN_TILE
    x = x_ref[...]
    h = _rms(x, g_ref[...])
    h_ref[...] = h.astype(BF16)
    rec = _router(h, wrh_ref, wrl_ref, br_ref)
    rec_hi, rec_lo = _split_bf16(rec)
    lane = lax.broadcasted_iota(jnp.int32, rec.shape, 1).astype(F32)
    gid = rec[:, ROUTER_GROUP_LANE:ROUTER_GROUP_LANE + 1]
    own = lane == gid
    before = jnp.dot(tri_ref[...], jnp.where(own, 1.0, 0.0).astype(BF16), preferred_element_type=F32)
    rank = jnp.sum(jnp.where(own, before, 0.0), axis=1, keepdims=True)
    totals = jnp.sum(jnp.where(own, 1.0, 0.0), axis=0, keepdims=True)
    acc_ref[...] = jnp.zeros_like(acc_ref)
    rh_ref[...] = rec_hi
    rl_ref[...] = rec_lo
    rank_ref[...] = jnp.where(lane == gid, rank, -1.0)
    for grp in range(MOE_GROUPS):
        cnt_ref[grp] = jnp.sum(jnp.where(lane[0:1, :] == grp, totals, 0.0)).astype(jnp.int32)
    contract0 = (((0,), (0,)), ((), ()))

    def one_chunk(k, grp):
        lane_t = lax.broadcasted_iota(jnp.int32, (tm, ROUTER_LANES), 1)
        rank_grp = jnp.sum(jnp.where(lane_t == grp, rank_ref[...], 0.0), axis=1, keepdims=True)
        slot = (lax.broadcasted_iota(jnp.int32, (tm, MOE_CHUNK), 1) + k * MOE_CHUNK).astype(F32)
        sel = jnp.where(rank_grp == slot, 1.0, 0.0).astype(BF16)
        xc = lax.dot_general(sel, h_ref[...], contract0, preferred_element_type=F32).astype(BF16)
        cw = (lax.dot_general(sel, rh_ref[...], contract0, preferred_element_type=F32)
              + lax.dot_general(sel, rl_ref[...], contract0, preferred_element_type=F32))
        lane_c = lax.broadcasted_iota(jnp.int32, cw.shape, 1)
        yc = jnp.zeros((MOE_CHUNK, D_MODEL), F32)
        for e in range(MOE_EXPERTS_PER_GROUP):
            ge = grp * MOE_EXPERTS_PER_GROUP + e
            a = jnp.dot(xc, w1_ref[ge], preferred_element_type=F32)
            b = jnp.dot(xc, w3_ref[ge], preferred_element_type=F32)
            c = jnp.sum(jnp.where(lane_c == ROUTER_E0 + ge, cw, 0.0), axis=1, keepdims=True)
            hid = (a * _sigmoid(a) * b * c).astype(BF16)
            yc = yc + jnp.dot(hid, w2_ref[ge], preferred_element_type=F32)
        acc_ref[...] += jnp.dot(sel, yc.astype(BF16), preferred_element_type=F32)
        return grp

    def one_group(grp, carry):
        n_chunks = (cnt_ref[grp] + MOE_CHUNK - 1) // MOE_CHUNK
        lax.fori_loop(0, n_chunks, one_chunk, grp)
        return carry

    lax.fori_loop(0, MOE_GROUPS, one_group, 0)
    y = x + acc_ref[...]
    o_ref[...] = _rms(y, fg_ref[...]) if final else y


def _moe(x, g, wr_hi, wr_lo, br, w1, w3, w2, final_g, final):
    n_tok = x.shape[0]
    tm = MOE_TOKEN_TILE
    row = lambda i: (i, 0)
    once = lambda shape: pl.BlockSpec(shape, lambda i: (0,) * len(shape), pipeline_mode=pl.Buffered(1))
    earlier = np.tril(np.ones((tm, tm), np.float32), -1)
    return pl.pallas_call(
        functools.partial(_moe_body, final=final),
        grid=(n_tok // tm,),
        in_specs=[pl.BlockSpec((tm, D_MODEL), row), _const_spec((1, D_MODEL)),
                  _const_spec(wr_hi.shape), _const_spec(wr_lo.shape), _const_spec(br.shape), once((tm, tm)),
                  once(w1.shape), once(w3.shape), once(w2.shape), _const_spec((1, D_MODEL))],
        out_specs=pl.BlockSpec((tm, D_MODEL), row),
        out_shape=jax.ShapeDtypeStruct((n_tok, D_MODEL), F32),
        scratch_shapes=[pltpu.VMEM((tm, D_MODEL), BF16), pltpu.VMEM((tm, D_MODEL), F32),
                        pltpu.VMEM((tm, ROUTER_LANES), BF16), pltpu.VMEM((tm, ROUTER_LANES), BF16),
                        pltpu.VMEM((tm, ROUTER_LANES), F32), pltpu.SMEM((MOE_GROUPS,), jnp.int32)],
        compiler_params=_cparams(("parallel",), 60),
        name="moe",
    )(x, g, wr_hi, wr_lo, br, jnp.asarray(earlier, BF16), w1, w3, w2, final_g)


def _rope_tables(seq):
    t = jnp.arange(seq)
    pos = jnp.stack([t // GRID_W, t % GRID_W], axis=-1).astype(F32)
    inv = ROPE_THETA ** (-jnp.arange(0, ROPE_AXIS_DIM, 2, dtype=F32) / ROPE_AXIS_DIM)
    ang = pos[..., None] * inv
    ang = jnp.concatenate([ang, ang], axis=-1).reshape(seq, HEAD_DIM)
    first = (jnp.arange(HEAD_DIM) % ROPE_AXIS_DIM) < (ROPE_AXIS_DIM // 2)
    cos = jnp.cos(ang)
    sin = jnp.where(first[None, :], -jnp.sin(ang), jnp.sin(ang))
    reps = QK_COLS // HEAD_DIM
    return jnp.tile(cos, (1, reps)), jnp.tile(sin, (1, reps))


ATT_HEAD_ORDER = (0, 2, 1, 3)


def _pack_layer(l, w_in, na_rpb, ml_gate_b, ml_norm_g, conv_dw_w, conv_dw_b, conv_norm_g, conv_norm_b,
                att_q_norm_g, att_k_norm_g, w_branch, w_gate, b_gate, w_out, moe_w_group, moe_b_group,
                moe_w_expert, moe_b_expert, moe_w1, moe_w3, moe_w2):
    scale = HEAD_DIM ** -0.5
    w = w_in[l]
    o0 = 0
    wna = w[:, o0:o0 + NA_COLS]
    wna = jnp.concatenate([wna[:, :BRANCH_WIDTH] * scale, wna[:, BRANCH_WIDTH:]], axis=1)
    o0 += NA_COLS
    wml = w[:, o0:o0 + 4 * BRANCH_WIDTH]
    wml = jnp.concatenate([wml[:, :BRANCH_WIDTH], wml[:, BRANCH_WIDTH:2 * BRANCH_WIDTH] * scale,
                           wml[:, 2 * BRANCH_WIDTH:]], axis=1)
    o0 += 4 * BRANCH_WIDTH
    n_gates = ML_N_GATES * N_HEADS
    wg = jnp.pad(w[:, o0:o0 + n_gates], ((0, 0), (0, LANES - n_gates)))
    o0 += n_gates
    wcv = w[:, o0:o0 + CONV_COLS]
    o0 += CONV_COLS
    wat = w[:, o0:o0 + ATT_COLS]
    wq = wat[:, :N_HEADS * HEAD_DIM].reshape(D_MODEL, N_HEADS, HEAD_DIM)[:, ATT_HEAD_ORDER, :]
    wat = jnp.concatenate([wq.reshape(D_MODEL, N_HEADS * HEAD_DIM), wat[:, N_HEADS * HEAD_DIM:]], axis=1)
    qkg = jnp.concatenate([jnp.tile(att_q_norm_g[l] * scale, N_HEADS), jnp.tile(att_k_norm_g[l], ATT_KV_HEADS)])
    wb = w_branch[l]
    wb3 = wb[3].reshape(N_HEADS, HEAD_DIM, D_MODEL)[ATT_HEAD_ORDER, :, :].reshape(BRANCH_WIDTH, D_MODEL)
    wb = jnp.concatenate([wb[:3], wb3[None]], axis=0)
    wr = jnp.concatenate([moe_w_group[l], moe_w_expert[l].reshape(D_MODEL, MOE_N_EXPERTS)], axis=1)
    n_r = MOE_GROUPS + MOE_N_EXPERTS
    br = jnp.concatenate([moe_b_group[l], moe_b_expert[l].reshape(MOE_N_EXPERTS)])
    wr_hi, wr_lo = _split_bf16(jnp.pad(wr, ((0, 0), (0, ROUTER_LANES - n_r))))
    return dict(
        wna=wna.astype(BF16), wml=wml.astype(BF16), wg=wg.astype(BF16), wcv=wcv.astype(BF16), wat=wat.astype(BF16),
        qkg=qkg.reshape(1, QK_COLS).astype(F32),
        na_bias=_na_bias_table(na_rpb[l]),
        gate_b=jnp.pad(ml_gate_b[l].reshape(1, n_gates), ((0, 0), (0, LANES - n_gates))).astype(F32),
        ml_norm_g=ml_norm_g[l].reshape(1, BRANCH_WIDTH),
        conv_w=jnp.broadcast_to(conv_dw_w[l][:, None, :], (CONV_WIDTH, SUBLANES, BRANCH_WIDTH)), conv_b=conv_dw_b[l].reshape(1, BRANCH_WIDTH),
        conv_g=conv_norm_g[l].reshape(1, BRANCH_WIDTH), conv_beta=conv_norm_b[l].reshape(1, BRANCH_WIDTH),
        w_gate=w_gate[l].astype(BF16), b_gate=b_gate[l].reshape(1, N_BRANCHES * D_MODEL),
        w_branch=wb.astype(BF16), w_out=w_out[l].astype(BF16),
        wr_hi=wr_hi, wr_lo=wr_lo, br=jnp.pad(br, (0, ROUTER_LANES - n_r)).reshape(1, ROUTER_LANES),
        w1=moe_w1[l].astype(BF16), w3=moe_w3[l].astype(BF16), w2=moe_w2[l].astype(BF16),
    )


def _trunk(x, layers, norm_mix_g, norm_ffn_g, final_norm_g, ones_qk, depth):
    batch, seq, _ = x.shape
    xf = x.reshape(batch * seq, D_MODEL)
    cos_t, sin_t = _rope_tables(seq)
    fg = final_norm_g.reshape(1, D_MODEL)
    for l in range(depth):
        p = layers[l]
        gm = norm_mix_g[l].reshape(1, D_MODEL)
        u_na, u_ml, gates, u_cv, qk, v = _inproj(xf, gm, p["wna"], p["wml"], p["wg"], p["wcv"], p["wat"], ones_qk,
                                                 p["qkg"], cos_t, sin_t, seq)
        o_na = _na(u_na, p["na_bias"], batch, seq)
        o_ml = _mlstm(u_ml, gates, p["gate_b"], p["ml_norm_g"], batch, seq)
        o_cv = _conv(u_cv, p["conv_w"], p["conv_b"], p["conv_g"], p["conv_beta"], batch, seq)
        o_at = _gqa(qk, v, batch, seq)
        xf = _merge(xf, gm, (o_na, o_ml, o_cv, o_at), p["w_gate"], p["b_gate"], p["w_branch"], p["w_out"])
        xf = _moe(xf, norm_ffn_g[l].reshape(1, D_MODEL), p["wr_hi"], p["wr_lo"], p["br"], p["w1"], p["w3"], p["w2"], fg,
                  final=(l == depth - 1))
    return xf.reshape(batch, seq, D_MODEL)


def kernel(x_prompt, x_sample, norm_mix_g, w_in, na_rpb, ml_gate_b, ml_norm_g, conv_dw_w, conv_dw_b, conv_norm_g, conv_norm_b, att_q_norm_g, att_k_norm_g, w_branch, w_gate, b_gate, w_out, norm_ffn_g, moe_w_group, moe_b_group, moe_w_expert, moe_b_expert, moe_w1, moe_w3, moe_w2, final_norm_g):
    depth = w_in.shape[0]
    layers = [_pack_layer(l, w_in, na_rpb, ml_gate_b, ml_norm_g, conv_dw_w, conv_dw_b, conv_norm_g, conv_norm_b,
                          att_q_norm_g, att_k_norm_g, w_branch, w_gate, b_gate, w_out, moe_w_group, moe_b_group,
                          moe_w_expert, moe_b_expert, moe_w1, moe_w3, moe_w2) for l in range(depth)]
    hid = np.arange(QK_COLS) // HEAD_DIM
    ones_qk = jnp.asarray(hid[:, None] == hid[None, :], BF16)
    y_prompt = _trunk(x_prompt, layers, norm_mix_g, norm_ffn_g, final_norm_g, ones_qk, depth)
    y_sample = _trunk(x_sample, layers, norm_mix_g, norm_ffn_g, final_norm_g, ones_qk, depth)
    return (y_prompt, y_sample)
```

```python
import functools

import jax
import jax.numpy as jnp
import numpy as np
from jax import lax
from jax.experimental import pallas as pl
from jax.experimental.pallas import tpu as pltpu

F32 = jnp.float32
BF16 = jnp.bfloat16

D_MODEL = 1024
GRID_W = 64
HEAD_DIM = 64
BRANCH_WIDTH = 256
N_BRANCHES = 4
N_HEADS = 4
NA_KH = 8
NA_KW = 16
ML_CHUNK = 64
ML_N_GATES = 4
CONV_WIDTH = 31
ATT_KV_HEADS = 2
ROPE_THETA = 10000.0
ROPE_AXIS_DIM = HEAD_DIM // 2
MOE_GROUPS = 4
MOE_EXPERTS_PER_GROUP = 4
MOE_N_EXPERTS = 16
MOE_D_EXPERT = 256
NORM_EPS = 1e-6
NEG_INF = -1e30

NA_COLS = 3 * BRANCH_WIDTH
ML_COLS = 4 * BRANCH_WIDTH + ML_N_GATES * N_HEADS
CONV_COLS = 2 * BRANCH_WIDTH
ATT_COLS = (N_HEADS + 2 * ATT_KV_HEADS) * HEAD_DIM
QK_COLS = (N_HEADS + ATT_KV_HEADS) * HEAD_DIM

LANES = 128
SUBLANES = 8
ROUTER_LANES = LANES
ROUTER_E0 = MOE_GROUPS
V7X_VMEM_BYTES = 64 * 1024 * 1024

TOKEN_TILE = 512


def _cparams(semantics, vmem_mb):
    assert vmem_mb * 1024 * 1024 < V7X_VMEM_BYTES
    return pltpu.CompilerParams(dimension_semantics=semantics, vmem_limit_bytes=vmem_mb * 1024 * 1024)


def _const_spec(shape):
    nd = len(shape)
    return pl.BlockSpec(shape, lambda *_: (0,) * nd)


def _rms(x, g):
    ms = jnp.mean(x * x, axis=-1, keepdims=True)
    return x * lax.rsqrt(ms + NORM_EPS) * g


def _inproj_body(x_ref, g_ref, wna_ref, wml_ref, wg_ref, wcv_ref, wat_ref, ones_ref, qkg_ref, cos_ref, sin_ref,
                 una_ref, uml_ref, gates_ref, ucv_ref, qk_ref, v_ref):
    h = _rms(x_ref[...], g_ref[...]).astype(BF16)
    una_ref[...] = jnp.dot(h, wna_ref[...], preferred_element_type=F32).astype(BF16)
    uml_ref[...] = jnp.dot(h, wml_ref[...], preferred_element_type=F32).astype(BF16)
    gates_ref[...] = jnp.dot(h, wg_ref[...], preferred_element_type=F32)
    ucv_ref[...] = jnp.dot(h, wcv_ref[...], preferred_element_type=F32).astype(BF16)
    ua = jnp.dot(h, wat_ref[...], preferred_element_type=F32)
    qk = ua[:, :QK_COLS]
    ms = jnp.dot((qk * qk).astype(BF16), ones_ref[...], preferred_element_type=F32) * (1.0 / HEAD_DIM)
    qn = qk * lax.rsqrt(ms + NORM_EPS) * qkg_ref[...]
    half = ROPE_AXIS_DIM // 2
    rot = []
    for c in range(QK_COLS // LANES):
        xc = qn[:, c * LANES:(c + 1) * LANES]
        lane = lax.broadcasted_iota(jnp.int32, xc.shape, 1)
        rot.append(jnp.where((lane % ROPE_AXIS_DIM) < half, pltpu.roll(xc, LANES - half, 1), pltpu.roll(xc, half, 1)))
    rot = jnp.concatenate(rot, axis=1)
    qk_ref[...] = (qn * cos_ref[...] + rot * sin_ref[...]).astype(BF16)
    v_ref[...] = ua[:, QK_COLS:].astype(BF16)


def _inproj(x, g, wna, wml, wg, wcv, wat, ones_qk, qkg, cos_t, sin_t, seq):
    n_tok = x.shape[0]
    tm = TOKEN_TILE
    tiles_per_seq = seq // tm
    row = lambda i: (i, 0)
    pos = lambda i: (i % tiles_per_seq, 0)
    outs = [
        jax.ShapeDtypeStruct((n_tok, NA_COLS), BF16),
        jax.ShapeDtypeStruct((n_tok, 4 * BRANCH_WIDTH), BF16),
        jax.ShapeDtypeStruct((n_tok, LANES), F32),
        jax.ShapeDtypeStruct((n_tok, CONV_COLS), BF16),
        jax.ShapeDtypeStruct((n_tok, QK_COLS), BF16),
        jax.ShapeDtypeStruct((n_tok, ATT_KV_HEADS * HEAD_DIM), BF16),
    ]
    return pl.pallas_call(
        _inproj_body,
        grid=(n_tok // tm,),
        in_specs=[
            pl.BlockSpec((tm, D_MODEL), row),
            _const_spec((1, D_MODEL)),
            _const_spec(wna.shape), _const_spec(wml.shape), _const_spec(wg.shape), _const_spec(wcv.shape),
            _const_spec(wat.shape), _const_spec(ones_qk.shape), _const_spec(qkg.shape),
            pl.BlockSpec((tm, QK_COLS), pos), pl.BlockSpec((tm, QK_COLS), pos),
        ],
        out_specs=[pl.BlockSpec((tm, o.shape[1]), row) for o in outs],
        out_shape=outs,
        compiler_params=_cparams(("parallel",), 48),
        name="inproj",
    )(x, g, wna, wml, wg, wcv, wat, ones_qk, qkg, cos_t, sin_t)


def _block_diag_mask(rows, cols):
    r = lax.broadcasted_iota(jnp.int32, (rows, cols), 0) // HEAD_DIM
    c = lax.broadcasted_iota(jnp.int32, (rows, cols), 1) // HEAD_DIM
    return r == (c % N_HEADS)


def _tile_heads(x):
    t = jnp.concatenate([x] * N_HEADS, axis=0)
    return jnp.where(_block_diag_mask(N_HEADS * HEAD_DIM, BRANCH_WIDTH), t, jnp.zeros_like(t))


def _na_body(q_ref, k_ref, v_ref, bias_ref, o_ref, *, rows):
    n_keys = NA_KH * GRID_W

    def one_row(r, carry):
        r0 = jnp.clip(r - NA_KH // 2, 0, rows - NA_KH)
        delta = r0 - r + NA_KH - 1
        q = q_ref[pl.ds(pl.multiple_of(r * GRID_W, GRID_W), GRID_W), :]
        kk = k_ref[pl.ds(pl.multiple_of(r0 * GRID_W, GRID_W), n_keys), :]
        vv = v_ref[pl.ds(pl.multiple_of(r0 * GRID_W, GRID_W), n_keys), :]
        s = lax.dot_general(_tile_heads(q), kk, (((1,), (1,)), ((), ())), preferred_element_type=F32)
        s = s + bias_ref[delta]
        m = jnp.max(s, axis=1, keepdims=True)
        p = jnp.exp(s - m)
        l = jnp.sum(p, axis=1, keepdims=True)
        of = jnp.dot(p.astype(BF16), vv, preferred_element_type=F32) / l
        of = jnp.where(_block_diag_mask(N_HEADS * HEAD_DIM, BRANCH_WIDTH), of, 0.0)
        o = of[0:HEAD_DIM] + of[HEAD_DIM:2 * HEAD_DIM] + of[2 * HEAD_DIM:3 * HEAD_DIM] + of[3 * HEAD_DIM:]
        o_ref[pl.ds(pl.multiple_of(r * GRID_W, GRID_W), GRID_W), :] = o.astype(BF16)
        return carry

    lax.fori_loop(0, rows, one_row, 0, unroll=4)


def _na_bias_table(rpb):
    qc = np.arange(GRID_W)[:, None]
    kc = np.arange(GRID_W)[None, :]
    win = np.clip(qc - NA_KW // 2, 0, GRID_W - NA_KW)
    in_win = (kc >= win) & (kc < win + NA_KW)
    col_off = np.clip(kc - qc + NA_KW - 1, 0, 2 * NA_KW - 2)
    n_ro, n_co = 2 * NA_KH - 1, 2 * NA_KW - 1
    pick = np.zeros((n_co, GRID_W * GRID_W), np.float32)
    pick[col_off.reshape(-1), np.arange(GRID_W * GRID_W)] = 1.0
    c = jnp.dot(rpb.astype(F32).reshape(N_HEADS * n_ro, n_co), pick, precision=lax.Precision.HIGHEST)
    c = jnp.where(in_win.reshape(1, -1), c, NEG_INF).reshape(N_HEADS, n_ro, GRID_W, GRID_W)
    c = c.transpose(0, 2, 1, 3)
    b = jnp.stack([c[:, :, d:d + NA_KH, :] for d in range(NA_KH)], axis=0)
    return b.reshape(NA_KH, N_HEADS * GRID_W, NA_KH * GRID_W)


def _na(u_na, bias, batch, seq):
    rows = seq // GRID_W
    return pl.pallas_call(
        functools.partial(_na_body, rows=rows),
        grid=(batch,),
        in_specs=[
            pl.BlockSpec((seq, BRANCH_WIDTH), lambda b: (b, 0)),
            pl.BlockSpec((seq, BRANCH_WIDTH), lambda b: (b, 1)),
            pl.BlockSpec((seq, BRANCH_WIDTH), lambda b: (b, 2)),
            _const_spec(bias.shape),
        ],
        out_specs=pl.BlockSpec((seq, BRANCH_WIDTH), lambda b: (b, 0)),
        out_shape=jax.ShapeDtypeStruct((batch * seq, BRANCH_WIDTH), BF16),
        compiler_params=_cparams(("parallel",), 48),
        name="na",
    )(u_na, u_na, u_na, bias)


def _expand_heads(cols):
    hid = lax.broadcasted_iota(jnp.int32, (ML_CHUNK, BRANCH_WIDTH), 1) // HEAD_DIM
    out = jnp.broadcast_to(cols[:, 0:1], (ML_CHUNK, BRANCH_WIDTH))
    for h in range(1, N_HEADS):
        out = jnp.where(hid == h, cols[:, h:h + 1], out)
    return out


def _head_rowmax(x):
    hid = lax.broadcasted_iota(jnp.int32, x.shape, 1) // HEAD_DIM
    out = jnp.zeros_like(x)
    for h in range(N_HEADS):
        mh = jnp.max(jnp.where(hid == h, x, NEG_INF), axis=1, keepdims=True)
        out = jnp.where(hid == h, mh, out)
    return out


def _mlstm_step_fn(reverse, row0, q_ref, k_ref, v_ref, gates_ref, gb_ref, h_ref, state_ref, n_chunks):
    L = ML_CHUNK
    W = BRANCH_WIDTH
    sub = lax.broadcasted_iota(jnp.int32, (L, W), 0)
    pos = lax.broadcasted_iota(jnp.int32, (L, W), 1) % HEAD_DIM
    diag = sub == pos
    causal = (pos >= sub) if reverse else (pos <= sub)
    tr = lax.broadcasted_iota(jnp.int32, (L, L), 0)
    tc = lax.broadcasted_iota(jnp.int32, (L, L), 1)
    tri = jnp.where((tc >= tr) if reverse else (tc <= tr), 1.0, 0.0).astype(BF16)
    ones_bd = _block_diag_mask(W, W).astype(BF16)
    bd2 = _block_diag_mask(W, 2 * W)
    i_off, f_off = (2 * N_HEADS, 3 * N_HEADS) if reverse else (0, N_HEADS)
    state_ref[...] = jnp.zeros_like(state_ref)

    def step(ci, m_run):
        c = (n_chunks - 1 - ci) if reverse else ci
        sl = pl.ds(pl.multiple_of(row0 + c * L, L), L)
        q = q_ref[sl, :]
        k = k_ref[sl, :]
        v = v_ref[sl, :]
        g = gates_ref[sl, :] + gb_ref[...]
        i_e = _expand_heads(g[:, i_off:i_off + N_HEADS])
        f_e = _expand_heads(jax.nn.log_sigmoid(g[:, f_off:f_off + N_HEADS]))
        f_hi = f_e.astype(BF16)
        f_r1 = f_e - f_hi.astype(F32)
        f_mid = f_r1.astype(BF16)
        f_lo = (f_r1 - f_mid.astype(F32)).astype(BF16)
        b = (jnp.dot(tri, f_hi, preferred_element_type=F32) + jnp.dot(tri, f_mid, preferred_element_type=F32)
             + jnp.dot(tri, f_lo, preferred_element_type=F32))
        b_end = b[0:1, :] if reverse else b[L - 1:L, :]
        rrow = jnp.sum(jnp.where(diag, i_e - b, 0.0), axis=0, keepdims=True)
        log_d = jnp.where(causal, b + rrow, NEG_INF)
        inter = b + m_run
        m_row = jnp.maximum(inter, _head_rowmax(log_d))
        d = jnp.exp(log_d - m_row)
        w_inter = jnp.exp(inter - m_row)
        s = lax.dot_general(q, _tile_heads(k), (((1,), (1,)), ((), ())), preferred_element_type=F32)
        p = (s * d).astype(BF16)
        r_loc = jnp.dot(p, jnp.concatenate([_tile_heads(v), ones_bd], axis=1), preferred_element_type=F32)
        r_int = jnp.dot(q, state_ref[...].astype(BF16), preferred_element_type=F32)
        num = r_loc[:, :W] + w_inter * r_int[:, :W]
        den = r_loc[:, W:] + w_inter * r_int[:, W:]
        h_ref[sl, :] = num / jnp.maximum(jnp.abs(den), jnp.exp(-m_row))
        log_w = b_end - b + i_e
        m_new = jnp.maximum(b_end + m_run, jnp.max(log_w, axis=0, keepdims=True))
        w = jnp.exp(log_w - m_new)
        decay = jnp.exp(b_end + m_run - m_new)
        upd_rhs = jnp.concatenate([(w * v.astype(F32)).astype(BF16), w.astype(BF16)], axis=1)
        upd = lax.dot_general(k, upd_rhs, (((0,), (0,)), ((), ())), preferred_element_type=F32)
        upd = jnp.where(bd2, upd, 0.0)
        state_ref[...] = jnp.concatenate([decay, decay], axis=1) * state_ref[...] + upd
        return m_new

    return step


ML_ROWS_PER_STEP = 8192


def _mlstm_body(q_ref, k_ref, v_ref, og_ref, gates_ref, gb_ref, ng_ref, o_ref, hf_ref, hb_ref, state_ref, *, seq, n_seq):
    n_chunks = seq // ML_CHUNK
    steps = []
    for j in range(n_seq):
        for reverse in (False, True):
            steps.append(_mlstm_step_fn(reverse, j * seq, q_ref, k_ref, v_ref, gates_ref, gb_ref,
                                        hb_ref if reverse else hf_ref, state_ref.at[2 * j + int(reverse)], n_chunks))
    m0 = jnp.full((1, BRANCH_WIDTH), NEG_INF, F32)
    lax.fori_loop(0, n_chunks, lambda ci, ms: tuple(step(ci, m) for step, m in zip(steps, ms)), (m0,) * len(steps))
    ones_bd = _block_diag_mask(BRANCH_WIDTH, BRANCH_WIDTH).astype(BF16)
    tile = 256

    def finish(t, carry):
        sl = pl.ds(pl.multiple_of(t * tile, tile), tile)
        hs = hf_ref[sl, :] + hb_ref[sl, :]
        ms = jnp.dot((hs * hs).astype(BF16), ones_bd, preferred_element_type=F32) * (1.0 / HEAD_DIM)
        hm = hs * lax.rsqrt(ms + NORM_EPS) * ng_ref[...]
        o_ref[sl, :] = (hm * _sigmoid(og_ref[sl, :].astype(F32))).astype(BF16)
        return carry

    lax.fori_loop(0, n_seq * seq // tile, finish, 0)


def _mlstm(u_ml, gates, gate_b, norm_g, batch, seq):
    n_seq = min(batch, ML_ROWS_PER_STEP // seq)
    rows = n_seq * seq
    col = lambda j: pl.BlockSpec((rows, BRANCH_WIDTH), lambda b: (b, j), pipeline_mode=pl.Buffered(1))
    return pl.pallas_call(
        functools.partial(_mlstm_body, seq=seq, n_seq=n_seq),
        grid=(batch // n_seq,),
        in_specs=[col(0), col(1), col(2), col(3),
                  pl.BlockSpec((rows, LANES), lambda b: (b, 0), pipeline_mode=pl.Buffered(1)),
                  _const_spec((1, LANES)), _const_spec((1, BRANCH_WIDTH))],
        out_specs=pl.BlockSpec((rows, BRANCH_WIDTH), lambda b: (b, 0)),
        out_shape=jax.ShapeDtypeStruct((batch * seq, BRANCH_WIDTH), BF16),
        scratch_shapes=[pltpu.VMEM((rows, BRANCH_WIDTH), F32), pltpu.VMEM((rows, BRANCH_WIDTH), F32),
                        pltpu.VMEM((2 * n_seq, BRANCH_WIDTH, 2 * BRANCH_WIDTH), F32)],
        compiler_params=_cparams(("parallel",), 56),
        name="mlstm",
    )(u_ml, u_ml, u_ml, u_ml, gates, gate_b, norm_g)


CONV_PAD = 16
CONV_TILE = 64


def _conv_body(u_ref, w_ref, b_ref, g_ref, beta_ref, o_ref, pad_ref, *, seq):
    u = u_ref[...]
    a = u[:, :BRANCH_WIDTH].astype(F32)
    gate = u[:, BRANCH_WIDTH:].astype(F32)
    zeros = jnp.zeros((CONV_PAD, BRANCH_WIDTH), F32)
    pad_ref[0:CONV_PAD, :] = zeros
    pad_ref[CONV_PAD + seq:, :] = zeros
    pad_ref[CONV_PAD:CONV_PAD + seq, :] = a * _sigmoid(gate)
    first = CONV_PAD - CONV_WIDTH // 2
    span = CONV_TILE + 2 * CONV_PAD - SUBLANES

    def tile(t, carry):
        base = pl.multiple_of(t * CONV_TILE, CONV_TILE)
        halves = []
        for c in range(BRANCH_WIDTH // LANES):
            lanes = slice(c * LANES, (c + 1) * LANES)
            win = pad_ref[pl.ds(base, CONV_TILE + 2 * CONV_PAD), lanes]
            n_blk = CONV_TILE // SUBLANES
            acc = jnp.zeros((n_blk, SUBLANES, LANES), F32)
            for r in range(SUBLANES):
                shifted = win[r:r + span, :].reshape(span // SUBLANES, SUBLANES, LANES)
                for j in range(CONV_WIDTH):
                    a8, rj = divmod(first + j, SUBLANES)
                    if rj == r:
                        acc = acc + shifted[a8:a8 + n_blk] * w_ref[j, :, lanes][None]
            halves.append(acc.reshape(CONV_TILE, LANES))
        y = jnp.concatenate(halves, axis=1) + b_ref[...]
        mu = jnp.mean(y, axis=-1, keepdims=True)
        yc = y - mu
        var = jnp.mean(yc * yc, axis=-1, keepdims=True)
        z = yc * lax.rsqrt(var + NORM_EPS) * g_ref[...] + beta_ref[...]
        o_ref[pl.ds(base, CONV_TILE), :] = (z * _sigmoid(z)).astype(BF16)
        return carry

    lax.fori_loop(0, seq // CONV_TILE, tile, 0, unroll=2)


def _conv(u_cv, w, b, g, beta, batch, seq):
    return pl.pallas_call(
        functools.partial(_conv_body, seq=seq),
        grid=(batch,),
        in_specs=[pl.BlockSpec((seq, CONV_COLS), lambda i: (i, 0)),
                  _const_spec(w.shape), _const_spec(b.shape), _const_spec(g.shape), _const_spec(beta.shape)],
        out_specs=pl.BlockSpec((seq, BRANCH_WIDTH), lambda i: (i, 0)),
        out_shape=jax.ShapeDtypeStruct((batch * seq, BRANCH_WIDTH), BF16),
        scratch_shapes=[pltpu.VMEM((seq + 2 * CONV_PAD, BRANCH_WIDTH), F32)],
        compiler_params=_cparams(("parallel",), 32),
        name="conv",
    )(u_cv, w, b, g, beta)


ATT_Q_TILE = 256


def _gqa_body(q_ref, k_ref, v_ref, o_ref):
    k = k_ref[...]
    v = v_ref[...]
    lane = lax.broadcasted_iota(jnp.int32, (ATT_Q_TILE, LANES), 1)
    low = lane < HEAD_DIM
    for c in range(2):
        qc = q_ref[:, c * LANES:(c + 1) * LANES]
        halves = []
        for keep in (low, jnp.logical_not(low)):
            qh = jnp.where(keep, qc, jnp.zeros_like(qc))
            s = lax.dot_general(qh, k, (((1,), (1,)), ((), ())), preferred_element_type=F32)
            m = jnp.max(s, axis=1, keepdims=True)
            p = jnp.exp(s - m)
            l = jnp.sum(p, axis=1, keepdims=True)
            halves.append(jnp.dot(p.astype(BF16), v, preferred_element_type=F32) / l)
        o_ref[:, c * LANES:(c + 1) * LANES] = jnp.where(low, halves[0], halves[1]).astype(BF16)


def _gqa(qk, v, batch, seq):
    tiles = seq // ATT_Q_TILE
    return pl.pallas_call(
        _gqa_body,
        grid=(batch, tiles),
        in_specs=[pl.BlockSpec((ATT_Q_TILE, N_HEADS * HEAD_DIM), lambda b, i: (b * tiles + i, 0)),
                  pl.BlockSpec((seq, LANES), lambda b, i: (b, 2)),
                  pl.BlockSpec((seq, LANES), lambda b, i: (b, 0))],
        out_specs=pl.BlockSpec((ATT_Q_TILE, BRANCH_WIDTH), lambda b, i: (b * tiles + i, 0)),
        out_shape=jax.ShapeDtypeStruct((batch * seq, BRANCH_WIDTH), BF16),
        compiler_params=_cparams(("parallel", "parallel"), 48),
        name="gqa",
    )(qk, qk, v)


def _sigmoid(x):
    return 0.5 * jnp.tanh(0.5 * x) + 0.5


def _merge_body(x_ref, g_ref, b0_ref, b1_ref, b2_ref, b3_ref, wg_ref, bg_ref, wb_ref, wo_ref, o_ref):
    x = x_ref[...]
    h = _rms(x, g_ref[...]).astype(BF16)
    acc = jnp.zeros(x.shape, F32)
    for n, br_ref in enumerate((b0_ref, b1_ref, b2_ref, b3_ref)):
        cols = slice(n * D_MODEL, (n + 1) * D_MODEL)
        gl = jnp.dot(h, wg_ref[:, cols], preferred_element_type=F32) + bg_ref[:, cols]
        pr = jnp.dot(br_ref[...], wb_ref[n], preferred_element_type=F32)
        acc = acc + _sigmoid(gl) * pr
    o_ref[...] = x + jnp.dot(acc.astype(BF16), wo_ref[...], preferred_element_type=F32)


def _merge(x, g, branches, w_gate, b_gate, w_branch, w_out):
    n_tok = x.shape[0]
    tm = TOKEN_TILE
    row = lambda i: (i, 0)
    return pl.pallas_call(
        _merge_body,
        grid=(n_tok // tm,),
        in_specs=[pl.BlockSpec((tm, D_MODEL), row), _const_spec((1, D_MODEL))]
        + [pl.BlockSpec((tm, BRANCH_WIDTH), row)] * N_BRANCHES
        + [_const_spec(w_gate.shape), _const_spec(b_gate.shape), _const_spec(w_branch.shape), _const_spec(w_out.shape)],
        out_specs=pl.BlockSpec((tm, D_MODEL), row),
        out_shape=jax.ShapeDtypeStruct((n_tok, D_MODEL), F32),
        compiler_params=_cparams(("parallel",), 56),
        name="merge",
    )(x, g, *branches, w_gate, b_gate, w_branch, w_out)


MOE_TOKEN_TILE = 1024
MOE_CHUNK = 320
ROUTER_GROUP_LANE = ROUTER_E0 + MOE_N_EXPERTS


def _split_bf16(x):
    hi = x.astype(BF16)
    return hi, (x - hi.astype(F32)).astype(BF16)


def _router(h, wr_hi_ref, wr_lo_ref, br_ref):
    h_hi, h_lo = _split_bf16(h)
    logits = (jnp.dot(h_hi, wr_hi_ref[...], preferred_element_type=F32)
              + jnp.dot(h_lo, wr_hi_ref[...], preferred_element_type=F32)
              + jnp.dot(h_hi, wr_lo_ref[...], preferred_element_type=F32)) + br_ref[...]
    lane = lax.broadcasted_iota(jnp.int32, logits.shape, 1).astype(F32)
    big = float(ROUTER_LANES)
    is_g = lane < MOE_GROUPS
    gl = jnp.where(is_g, logits, NEG_INF)
    g_max = jnp.max(gl, axis=1, keepdims=True)
    g_idx = jnp.min(jnp.where(is_g & (gl == g_max), lane, big), axis=1, keepdims=True)
    g_gate = 1.0 / jnp.sum(jnp.where(is_g, jnp.exp(gl - g_max), 0.0), axis=1, keepdims=True)
    e_lo = ROUTER_E0 + g_idx * MOE_EXPERTS_PER_GROUP
    in_grp = (lane >= e_lo) & (lane < e_lo + MOE_EXPERTS_PER_GROUP)
    el = jnp.where(in_grp, logits, NEG_INF)
    top1 = jnp.max(el, axis=1, keepdims=True)
    idx1 = jnp.min(jnp.where(in_grp & (el == top1), lane, big), axis=1, keepdims=True)
    rest = in_grp & (lane != idx1)
    el2 = jnp.where(rest, logits, NEG_INF)
    top2 = jnp.max(el2, axis=1, keepdims=True)
    idx2 = jnp.min(jnp.where(rest & (el2 == top2), lane, big), axis=1, keepdims=True)
    e2 = jnp.exp(top2 - top1)
    w1 = g_gate / (1.0 + e2)
    w2 = g_gate * e2 / (1.0 + e2)
    rec = jnp.where(lane == idx1, w1, 0.0) + jnp.where(lane == idx2, w2, 0.0)
    return jnp.where(lane == ROUTER_GROUP_LANE, g_idx, rec)


def _moe_body(x_ref, g_ref, wrh_ref, wrl_ref, br_ref, tri_ref, w1_ref, w3_ref, w2_ref, fg_ref, o_ref,
              h_ref, acc_ref, rh_ref, rl_ref, rank_ref, cnt_ref, *, final):
    tm = MOE_TOKEN_TILE
    x = x_ref[...]
    h = _rms(x, g_ref[...])
    h_ref[...] = h.astype(BF16)
    rec = _router(h, wrh_ref, wrl_ref, br_ref)
    rec_hi, rec_lo = _split_bf16(rec)
    lane = lax.broadcasted_iota(jnp.int32, rec.shape, 1).astype(F32)
    gid = rec[:, ROUTER_GROUP_LANE:ROUTER_GROUP_LANE + 1]
    own = lane == gid
    before = jnp.dot(tri_ref[...], jnp.where(own, 1.0, 0.0).astype(BF16), preferred_element_type=F32)
    rank = jnp.sum(jnp.where(own, before, 0.0), axis=1, keepdims=True)
    totals = jnp.sum(jnp.where(own, 1.0, 0.0), axis=0, keepdims=True)
    acc_ref[...] = jnp.zeros_like(acc_ref)
    rh_ref[...] = rec_hi
    rl_ref[...] = rec_lo
    rank_ref[...] = jnp.where(lane == gid, rank, -1.0)
    for grp in range(MOE_GROUPS):
        cnt_ref[grp] = jnp.sum(jnp.where(lane[0:1, :] == grp, totals, 0.0)).astype(jnp.int32)
    contract0 = (((0,), (0,)), ((), ()))

    def one_chunk(k, grp):
        lane_t = lax.broadcasted_iota(jnp.int32, (tm, ROUTER_LANES), 1)
        rank_grp = jnp.sum(jnp.where(lane_t == grp, rank_ref[...], 0.0), axis=1, keepdims=True)
        slot = (lax.broadcasted_iota(jnp.int32, (tm, MOE_CHUNK), 1) + k * MOE_CHUNK).astype(F32)
        sel = jnp.where(rank_grp == slot, 1.0, 0.0).astype(BF16)
        xc = lax.dot_general(sel, h_ref[...], contract0, preferred_element_type=F32).astype(BF16)
        cw = (lax.dot_general(sel, rh_ref[...], contract0, preferred_element_type=F32)
              + lax.dot_general(sel, rl_ref[...], contract0, preferred_element_type=F32))
        lane_c = lax.broadcasted_iota(jnp.int32, cw.shape, 1)
        yc = jnp.zeros((MOE_CHUNK, D_MODEL), F32)
        for e in range(MOE_EXPERTS_PER_GROUP):
            ge = grp * MOE_EXPERTS_PER_GROUP + e
            a = jnp.dot(xc, w1_ref[ge], preferred_element_type=F32)
            b = jnp.dot(xc, w3_ref[ge], preferred_element_type=F32)
            c = jnp.sum(jnp.where(lane_c == ROUTER_E0 + ge, cw, 0.0), axis=1, keepdims=True)
            hid = (a * _sigmoid(a) * b * c).astype(BF16)
            yc = yc + jnp.dot(hid, w2_ref[ge], preferred_element_type=F32)
        acc_ref[...] += jnp.dot(sel, yc.astype(BF16), preferred_element_type=F32)
        return grp

    def one_group(grp, carry):
        n_chunks = (cnt_ref[grp] + MOE_CHUNK - 1) // MOE_CHUNK
        lax.fori_loop(0, n_chunks, one_chunk, grp)
        return carry

    lax.fori_loop(0, MOE_GROUPS, one_group, 0)
    y = x + acc_ref[...]
    o_ref[...] = _rms(y, fg_ref[...]) if final else y


def _moe(x, g, wr_hi, wr_lo, br, w1, w3, w2, final_g, final):
    n_tok = x.shape[0]
    tm = MOE_TOKEN_TILE
    row = lambda i: (i, 0)
    once = lambda shape: pl.BlockSpec(shape, lambda i: (0,) * len(shape), pipeline_mode=pl.Buffered(1))
    earlier = np.tril(np.ones((tm, tm), np.float32), -1)
    return pl.pallas_call(
        functools.partial(_moe_body, final=final),
        grid=(n_tok // tm,),
        in_specs=[pl.BlockSpec((tm, D_MODEL), row), _const_spec((1, D_MODEL)),
                  _const_spec(wr_hi.shape), _const_spec(wr_lo.shape), _const_spec(br.shape), once((tm, tm)),
                  once(w1.shape), once(w3.shape), once(w2.shape), _const_spec((1, D_MODEL))],
        out_specs=pl.BlockSpec((tm, D_MODEL), row),
        out_shape=jax.ShapeDtypeStruct((n_tok, D_MODEL), F32),
        scratch_shapes=[pltpu.VMEM((tm, D_MODEL), BF16), pltpu.VMEM((tm, D_MODEL), F32),
                        pltpu.VMEM((tm, ROUTER_LANES), BF16), pltpu.VMEM((tm, ROUTER_LANES), BF16),
                        pltpu.VMEM((tm, ROUTER_LANES), F32), pltpu.SMEM((MOE_GROUPS,), jnp.int32)],
        compiler_params=_cparams(("parallel",), 60),
        name="moe",
    )(x, g, wr_hi, wr_lo, br, jnp.asarray(earlier, BF16), w1, w3, w2, final_g)


def _rope_tables(seq):
    t = jnp.arange(seq)
    pos = jnp.stack([t // GRID_W, t % GRID_W], axis=-1).astype(F32)
    inv = ROPE_THETA ** (-jnp.arange(0, ROPE_AXIS_DIM, 2, dtype=F32) / ROPE_AXIS_DIM)
    ang = pos[..., None] * inv
    ang = jnp.concatenate([ang, ang], axis=-1).reshape(seq, HEAD_DIM)
    first = (jnp.arange(HEAD_DIM) % ROPE_AXIS_DIM) < (ROPE_AXIS_DIM // 2)
    cos = jnp.cos(ang)
    sin = jnp.where(first[None, :], -jnp.sin(ang), jnp.sin(ang))
    reps = QK_COLS // HEAD_DIM
    return jnp.tile(cos, (1, reps)), jnp.tile(sin, (1, reps))


ATT_HEAD_ORDER = (0, 2, 1, 3)


def _pack_layer(l, w_in, na_rpb, ml_gate_b, ml_norm_g, conv_dw_w, conv_dw_b, conv_norm_g, conv_norm_b,
                att_q_norm_g, att_k_norm_g, w_branch, w_gate, b_gate, w_out, moe_w_group, moe_b_group,
                moe_w_expert, moe_b_expert, moe_w1, moe_w3, moe_w2):
    scale = HEAD_DIM ** -0.5
    w = w_in[l]
    o0 = 0
    wna = w[:, o0:o0 + NA_COLS]
    wna = jnp.concatenate([wna[:, :BRANCH_WIDTH] * scale, wna[:, BRANCH_WIDTH:]], axis=1)
    o0 += NA_COLS
    wml = w[:, o0:o0 + 4 * BRANCH_WIDTH]
    wml = jnp.concatenate([wml[:, :BRANCH_WIDTH], wml[:, BRANCH_WIDTH:2 * BRANCH_WIDTH] * scale,
                           wml[:, 2 * BRANCH_WIDTH:]], axis=1)
    o0 += 4 * BRANCH_WIDTH
    n_gates = ML_N_GATES * N_HEADS
    wg = jnp.pad(w[:, o0:o0 + n_gates], ((0, 0), (0, LANES - n_gates)))
    o0 += n_gates
    wcv = w[:, o0:o0 + CONV_COLS]
    o0 += CONV_COLS
    wat = w[:, o0:o0 + ATT_COLS]
    wq = wat[:, :N_HEADS * HEAD_DIM].reshape(D_MODEL, N_HEADS, HEAD_DIM)[:, ATT_HEAD_ORDER, :]
    wat = jnp.concatenate([wq.reshape(D_MODEL, N_HEADS * HEAD_DIM), wat[:, N_HEADS * HEAD_DIM:]], axis=1)
    qkg = jnp.concatenate([jnp.tile(att_q_norm_g[l] * scale, N_HEADS), jnp.tile(att_k_norm_g[l], ATT_KV_HEADS)])
    wb = w_branch[l]
    wb3 = wb[3].reshape(N_HEADS, HEAD_DIM, D_MODEL)[ATT_HEAD_ORDER, :, :].reshape(BRANCH_WIDTH, D_MODEL)
    wb = jnp.concatenate([wb[:3], wb3[None]], axis=0)
    wr = jnp.concatenate([moe_w_group[l], moe_w_expert[l].reshape(D_MODEL, MOE_N_EXPERTS)], axis=1)
    n_r = MOE_GROUPS + MOE_N_EXPERTS
    br = jnp.concatenate([moe_b_group[l], moe_b_expert[l].reshape(MOE_N_EXPERTS)])
    wr_hi, wr_lo = _split_bf16(jnp.pad(wr, ((0, 0), (0, ROUTER_LANES - n_r))))
    return dict(
        wna=wna.astype(BF16), wml=wml.astype(BF16), wg=wg.astype(BF16), wcv=wcv.astype(BF16), wat=wat.astype(BF16),
        qkg=qkg.reshape(1, QK_COLS).astype(F32),
        na_bias=_na_bias_table(na_rpb[l]),
        gate_b=jnp.pad(ml_gate_b[l].reshape(1, n_gates), ((0, 0), (0, LANES - n_gates))).astype(F32),
        ml_norm_g=ml_norm_g[l].reshape(1, BRANCH_WIDTH),
        conv_w=jnp.broadcast_to(conv_dw_w[l][:, None, :], (CONV_WIDTH, SUBLANES, BRANCH_WIDTH)), conv_b=conv_dw_b[l].reshape(1, BRANCH_WIDTH),
        conv_g=conv_norm_g[l].reshape(1, BRANCH_WIDTH), conv_beta=conv_norm_b[l].reshape(1, BRANCH_WIDTH),
        w_gate=w_gate[l].astype(BF16), b_gate=b_gate[l].reshape(1, N_BRANCHES * D_MODEL),
        w_branch=wb.astype(BF16), w_out=w_out[l].astype(BF16),
        wr_hi=wr_hi, wr_lo=wr_lo, br=jnp.pad(br, (0, ROUTER_LANES - n_r)).reshape(1, ROUTER_LANES),
        w1=moe_w1[l].astype(BF16), w3=moe_w3[l].astype(BF16), w2=moe_w2[l].astype(BF16),
    )


def _trunk(x, layers, norm_mix_g, norm_ffn_g, final_norm_g, ones_qk, depth):
    batch, seq, _ = x.shape
    xf = x.reshape(batch * seq, D_MODEL)
    cos_t, sin_t = _rope_tables(seq)
    fg = final_norm_g.reshape(1, D_MODEL)
    for l in range(depth):
        p = layers[l]
        gm = norm_mix_g[l].reshape(1, D_MODEL)
        u_na, u_ml, gates, u_cv, qk, v = _inproj(xf, gm, p["wna"], p["wml"], p["wg"], p["wcv"], p["wat"], ones_qk,
                                                 p["qkg"], cos_t, sin_t, seq)
        o_na = _na(u_na, p["na_bias"], batch, seq)
        o_ml = _mlstm(u_ml, gates, p["gate_b"], p["ml_norm_g"], batch, seq)
        o_cv = _conv(u_cv, p["conv_w"], p["conv_b"], p["conv_g"], p["conv_beta"], batch, seq)
        o_at = _gqa(qk, v, batch, seq)
        xf = _merge(xf, gm, (o_na, o_ml, o_cv, o_at), p["w_gate"], p["b_gate"], p["w_branch"], p["w_out"])
        xf = _moe(xf, norm_ffn_g[l].reshape(1, D_MODEL), p["wr_hi"], p["wr_lo"], p["br"], p["w1"], p["w3"], p["w2"], fg,
                  final=(l == depth - 1))
    return xf.reshape(batch, seq, D_MODEL)


def kernel(x_prompt, x_sample, norm_mix_g, w_in, na_rpb, ml_gate_b, ml_norm_g, conv_dw_w, conv_dw_b, conv_norm_g, conv_norm_b, att_q_norm_g, att_k_norm_g, w_branch, w_gate, b_gate, w_out, norm_ffn_g, moe_w_group, moe_b_group, moe_w_expert, moe_b_expert, moe_w1, moe_w3, moe_w2, final_norm_g):
    depth = w_in.shape[0]
    layers = [_pack_layer(l, w_in, na_rpb, ml_gate_b, ml_norm_g, conv_dw_w, conv_dw_b, conv_norm_g, conv_norm_b,
                          att_q_norm_g, att_k_norm_g, w_branch, w_gate, b_gate, w_out, moe_w_group, moe_b_group,
                          moe_w_expert, moe_b_expert, moe_w1, moe_w3, moe_w2) for l in range(depth)]
    hid = np.arange(QK_COLS) // HEAD_DIM
    ones_qk = jnp.asarray(hid[:, None] == hid[None, :], BF16)
    y_prompt = _trunk(x_prompt, layers, norm_mix_g, norm_ffn_g, final_norm_g, ones_qk, depth)
    y_sample = _trunk(x_sample, layers, norm_mix_g, norm_ffn_g, final_norm_g, ones_qk, depth)
    return (y_prompt, y_sample)
```

```python
import functools

import jax
import jax.numpy as jnp
import numpy as np
from jax import lax
from jax.experimental import pallas as pl
from jax.experimental.pallas import tpu as pltpu

F32 = jnp.float32
BF16 = jnp.bfloat16

D_MODEL = 1024
GRID_W = 64
HEAD_DIM = 64
BRANCH_WIDTH = 256
N_BRANCHES = 4
N_HEADS = 4
NA_KH = 8
NA_KW = 16
ML_CHUNK = 64
ML_N_GATES = 4
CONV_WIDTH = 31
ATT_KV_HEADS = 2
ROPE_THETA = 10000.0
ROPE_AXIS_DIM = HEAD_DIM // 2
MOE_GROUPS = 4
MOE_EXPERTS_PER_GROUP = 4
MOE_N_EXPERTS = 16
MOE_D_EXPERT = 256
NORM_EPS = 1e-6
NEG_INF = -1e30

NA_COLS = 3 * BRANCH_WIDTH
ML_COLS = 4 * BRANCH_WIDTH + ML_N_GATES * N_HEADS
CONV_COLS = 2 * BRANCH_WIDTH
ATT_COLS = (N_HEADS + 2 * ATT_KV_HEADS) * HEAD_DIM
QK_COLS = (N_HEADS + ATT_KV_HEADS) * HEAD_DIM

LANES = 128
SUBLANES = 8
ROUTER_LANES = LANES
ROUTER_E0 = MOE_GROUPS
V7X_VMEM_BYTES = 64 * 1024 * 1024

TOKEN_TILE = 512


def _cparams(semantics, vmem_mb):
    assert vmem_mb * 1024 * 1024 < V7X_VMEM_BYTES
    return pltpu.CompilerParams(dimension_semantics=semantics, vmem_limit_bytes=vmem_mb * 1024 * 1024)


def _const_spec(shape):
    nd = len(shape)
    return pl.BlockSpec(shape, lambda *_: (0,) * nd)


def _resident_spec(shape):
    nd = len(shape)
    return pl.BlockSpec(shape, lambda *_: (0,) * nd, pipeline_mode=pl.Buffered(1))


def _rms(x, g):
    ms = jnp.mean(x * x, axis=-1, keepdims=True)
    return x * lax.rsqrt(ms + NORM_EPS) * g


def _inproj_body(x_ref, g_ref, wna_ref, wml_ref, wg_ref, wcv_ref, wat_ref, ones_ref, qkg_ref, cos_ref, sin_ref,
                 una_ref, uml_ref, gates_ref, ucv_ref, qk_ref, v_ref):
    h = _rms(x_ref[...], g_ref[...]).astype(BF16)
    una_ref[...] = jnp.dot(h, wna_ref[...], preferred_element_type=F32).astype(BF16)
    uml_ref[...] = jnp.dot(h, wml_ref[...], preferred_element_type=F32).astype(BF16)
    gates_ref[...] = jnp.dot(h, wg_ref[...], preferred_element_type=F32)
    ucv_ref[...] = jnp.dot(h, wcv_ref[...], preferred_element_type=F32).astype(BF16)
    ua = jnp.dot(h, wat_ref[...], preferred_element_type=F32)
    qk = ua[:, :QK_COLS]
    ms = jnp.dot((qk * qk).astype(BF16), ones_ref[...], preferred_element_type=F32) * (1.0 / HEAD_DIM)
    qn = qk * lax.rsqrt(ms + NORM_EPS) * qkg_ref[...]
    half = ROPE_AXIS_DIM // 2
    rot = []
    for c in range(QK_COLS // LANES):
        xc = qn[:, c * LANES:(c + 1) * LANES]
        lane = lax.broadcasted_iota(jnp.int32, xc.shape, 1)
        rot.append(jnp.where((lane % ROPE_AXIS_DIM) < half, pltpu.roll(xc, LANES - half, 1), pltpu.roll(xc, half, 1)))
    rot = jnp.concatenate(rot, axis=1)
    qk_ref[...] = (qn * cos_ref[...] + rot * sin_ref[...]).astype(BF16)
    v_ref[...] = ua[:, QK_COLS:].astype(BF16)


def _inproj(x, g, wna, wml, wg, wcv, wat, ones_qk, qkg, cos_t, sin_t, seq):
    n_tok = x.shape[0]
    tm = 2 * TOKEN_TILE
    tiles_per_seq = seq // tm
    row = lambda i: (i, 0)
    pos = lambda i: (i % tiles_per_seq, 0)
    outs = [
        jax.ShapeDtypeStruct((n_tok, NA_COLS), BF16),
        jax.ShapeDtypeStruct((n_tok, 4 * BRANCH_WIDTH), BF16),
        jax.ShapeDtypeStruct((n_tok, LANES), F32),
        jax.ShapeDtypeStruct((n_tok, CONV_COLS), BF16),
        jax.ShapeDtypeStruct((n_tok, QK_COLS), BF16),
        jax.ShapeDtypeStruct((n_tok, ATT_KV_HEADS * HEAD_DIM), BF16),
    ]
    return pl.pallas_call(
        _inproj_body,
        grid=(n_tok // tm,),
        in_specs=[
            pl.BlockSpec((tm, D_MODEL), row),
            _const_spec((1, D_MODEL)),
            _resident_spec(wna.shape), _resident_spec(wml.shape), _const_spec(wg.shape), _resident_spec(wcv.shape),
            _resident_spec(wat.shape), _const_spec(ones_qk.shape), _const_spec(qkg.shape),
            pl.BlockSpec((tm, QK_COLS), pos), pl.BlockSpec((tm, QK_COLS), pos),
        ],
        out_specs=[pl.BlockSpec((tm, o.shape[1]), row) for o in outs],
        out_shape=outs,
        compiler_params=_cparams(("parallel",), 48),
        name="inproj",
    )(x, g, wna, wml, wg, wcv, wat, ones_qk, qkg, cos_t, sin_t)


def _block_diag_mask(rows, cols):
    r = lax.broadcasted_iota(jnp.int32, (rows, cols), 0) // HEAD_DIM
    c = lax.broadcasted_iota(jnp.int32, (rows, cols), 1) // HEAD_DIM
    return r == (c % N_HEADS)


def _tile_heads(x):
    t = jnp.concatenate([x] * N_HEADS, axis=0)
    return jnp.where(_block_diag_mask(N_HEADS * HEAD_DIM, BRANCH_WIDTH), t, jnp.zeros_like(t))


def _na_body(q_ref, k_ref, v_ref, bias_ref, o_ref, *, rows):
    n_keys = NA_KH * GRID_W

    def one_row(r, carry):
        r0 = jnp.clip(r - NA_KH // 2, 0, rows - NA_KH)
        delta = r0 - r + NA_KH - 1
        q = q_ref[pl.ds(pl.multiple_of(r * GRID_W, GRID_W), GRID_W), :]
        kk = k_ref[pl.ds(pl.multiple_of(r0 * GRID_W, GRID_W), n_keys), :]
        vv = v_ref[pl.ds(pl.multiple_of(r0 * GRID_W, GRID_W), n_keys), :]
        s = lax.dot_general(_tile_heads(q), kk, (((1,), (1,)), ((), ())), preferred_element_type=F32)
        s = s + bias_ref[delta]
        m = jnp.max(s, axis=1, keepdims=True)
        p = jnp.exp(s - m)
        l = jnp.sum(p, axis=1, keepdims=True)
        of = jnp.dot(p.astype(BF16), vv, preferred_element_type=F32) / l
        of = jnp.where(_block_diag_mask(N_HEADS * HEAD_DIM, BRANCH_WIDTH), of, 0.0)
        o = of[0:HEAD_DIM] + of[HEAD_DIM:2 * HEAD_DIM] + of[2 * HEAD_DIM:3 * HEAD_DIM] + of[3 * HEAD_DIM:]
        o_ref[pl.ds(pl.multiple_of(r * GRID_W, GRID_W), GRID_W), :] = o.astype(BF16)
        return carry

    lax.fori_loop(0, rows, one_row, 0, unroll=4)


def _na_bias_table(rpb):
    qc = np.arange(GRID_W)[:, None]
    kc = np.arange(GRID_W)[None, :]
    win = np.clip(qc - NA_KW // 2, 0, GRID_W - NA_KW)
    in_win = (kc >= win) & (kc < win + NA_KW)
    col_off = np.clip(kc - qc + NA_KW - 1, 0, 2 * NA_KW - 2)
    n_ro, n_co = 2 * NA_KH - 1, 2 * NA_KW - 1
    pick = np.zeros((n_co, GRID_W * GRID_W), np.float32)
    pick[col_off.reshape(-1), np.arange(GRID_W * GRID_W)] = 1.0
    c = jnp.dot(rpb.astype(F32).reshape(N_HEADS * n_ro, n_co), pick, precision=lax.Precision.HIGHEST)
    c = jnp.where(in_win.reshape(1, -1), c, NEG_INF).reshape(N_HEADS, n_ro, GRID_W, GRID_W)
    c = c.transpose(0, 2, 1, 3)
    b = jnp.stack([c[:, :, d:d + NA_KH, :] for d in range(NA_KH)], axis=0)
    return b.reshape(NA_KH, N_HEADS * GRID_W, NA_KH * GRID_W)


def _na(u_na, bias, batch, seq):
    rows = seq // GRID_W
    return pl.pallas_call(
        functools.partial(_na_body, rows=rows),
        grid=(batch,),
        in_specs=[
            pl.BlockSpec((seq, BRANCH_WIDTH), lambda b: (b, 0)),
            pl.BlockSpec((seq, BRANCH_WIDTH), lambda b: (b, 1)),
            pl.BlockSpec((seq, BRANCH_WIDTH), lambda b: (b, 2)),
            _const_spec(bias.shape),
        ],
        out_specs=pl.BlockSpec((seq, BRANCH_WIDTH), lambda b: (b, 0)),
        out_shape=jax.ShapeDtypeStruct((batch * seq, BRANCH_WIDTH), BF16),
        compiler_params=_cparams(("parallel",), 48),
        name="na",
    )(u_na, u_na, u_na, bias)


def _expand_heads(cols):
    hid = lax.broadcasted_iota(jnp.int32, (ML_CHUNK, BRANCH_WIDTH), 1) // HEAD_DIM
    out = jnp.broadcast_to(cols[:, 0:1], (ML_CHUNK, BRANCH_WIDTH))
    for h in range(1, N_HEADS):
        out = jnp.where(hid == h, cols[:, h:h + 1], out)
    return out


def _head_rowmax(x):
    hid = lax.broadcasted_iota(jnp.int32, x.shape, 1) // HEAD_DIM
    out = jnp.zeros_like(x)
    for h in range(N_HEADS):
        mh = jnp.max(jnp.where(hid == h, x, NEG_INF), axis=1, keepdims=True)
        out = jnp.where(hid == h, mh, out)
    return out


def _mlstm_step_fn(reverse, row0, q_ref, k_ref, v_ref, gates_ref, gb_ref, h_ref, state_ref, n_chunks):
    L = ML_CHUNK
    W = BRANCH_WIDTH
    sub = lax.broadcasted_iota(jnp.int32, (L, W), 0)
    pos = lax.broadcasted_iota(jnp.int32, (L, W), 1) % HEAD_DIM
    diag = sub == pos
    causal = (pos >= sub) if reverse else (pos <= sub)
    tr = lax.broadcasted_iota(jnp.int32, (L, L), 0)
    tc = lax.broadcasted_iota(jnp.int32, (L, L), 1)
    tri = jnp.where((tc >= tr) if reverse else (tc <= tr), 1.0, 0.0).astype(BF16)
    ones_bd = _block_diag_mask(W, W).astype(BF16)
    bd = _block_diag_mask(W, W)
    i_off, f_off = (2 * N_HEADS, 3 * N_HEADS) if reverse else (0, N_HEADS)
    state_ref[...] = jnp.zeros_like(state_ref)

    def step(ci, carry):
        m_run, n_run = carry
        c = (n_chunks - 1 - ci) if reverse else ci
        sl = pl.ds(pl.multiple_of(row0 + c * L, L), L)
        q = q_ref[sl, :]
        k = k_ref[sl, :]
        v = v_ref[sl, :]
        g = gates_ref[sl, :] + gb_ref[...]
        i_e = _expand_heads(g[:, i_off:i_off + N_HEADS])
        f_e = _expand_heads(jax.nn.log_sigmoid(g[:, f_off:f_off + N_HEADS]))
        f_hi = f_e.astype(BF16)
        f_r1 = f_e - f_hi.astype(F32)
        f_mid = f_r1.astype(BF16)
        f_lo = (f_r1 - f_mid.astype(F32)).astype(BF16)
        b = (jnp.dot(tri, f_hi, preferred_element_type=F32) + jnp.dot(tri, f_mid, preferred_element_type=F32)
             + jnp.dot(tri, f_lo, preferred_element_type=F32))
        b_end = b[0:1, :] if reverse else b[L - 1:L, :]
        rrow = jnp.sum(jnp.where(diag, i_e - b, 0.0), axis=0, keepdims=True)
        log_d = jnp.where(causal, b + rrow, NEG_INF)
        inter = b + m_run
        m_row = jnp.maximum(inter, _head_rowmax(log_d))
        d = jnp.exp(log_d - m_row)
        w_inter = jnp.exp(inter - m_row)
        s = lax.dot_general(q, _tile_heads(k), (((1,), (1,)), ((), ())), preferred_element_type=F32)
        p = (s * d).astype(BF16)
        r_loc = jnp.dot(p, jnp.concatenate([_tile_heads(v), ones_bd], axis=1), preferred_element_type=F32)
        num_int = jnp.dot(q, state_ref[...].astype(BF16), preferred_element_type=F32)
        den_int = jnp.dot((q.astype(F32) * n_run).astype(BF16), ones_bd, preferred_element_type=F32)
        num = r_loc[:, :W] + w_inter * num_int
        den = r_loc[:, W:] + w_inter * den_int
        h_ref[sl, :] = num / jnp.maximum(jnp.abs(den), jnp.exp(-m_row))
        log_w = b_end - b + i_e
        m_new = jnp.maximum(b_end + m_run, jnp.max(log_w, axis=0, keepdims=True))
        w = jnp.exp(log_w - m_new)
        decay = jnp.exp(b_end + m_run - m_new)
        upd = lax.dot_general(k, (w * v.astype(F32)).astype(BF16), (((0,), (0,)), ((), ())),
                              preferred_element_type=F32)
        state_ref[...] = decay * state_ref[...] + jnp.where(bd, upd, 0.0)
        n_new = decay * n_run + jnp.sum(w * k.astype(F32), axis=0, keepdims=True)
        return m_new, n_new

    return step


ML_ROWS_PER_STEP = 8192


def _mlstm_body(q_ref, k_ref, v_ref, og_ref, gates_ref, gb_ref, ng_ref, o_ref, hf_ref, hb_ref, state_ref, *, seq, n_seq):
    n_chunks = seq // ML_CHUNK
    steps = []
    for j in range(n_seq):
        for reverse in (False, True):
            steps.append(_mlstm_step_fn(reverse, j * seq, q_ref, k_ref, v_ref, gates_ref, gb_ref,
                                        hb_ref if reverse else hf_ref, state_ref.at[2 * j + int(reverse)], n_chunks))
    start = (jnp.full((1, BRANCH_WIDTH), NEG_INF, F32), jnp.zeros((1, BRANCH_WIDTH), F32))
    lax.fori_loop(0, n_chunks, lambda ci, cs: tuple(step(ci, c) for step, c in zip(steps, cs)), (start,) * len(steps))
    ones_bd = _block_diag_mask(BRANCH_WIDTH, BRANCH_WIDTH).astype(BF16)
    tile = 256

    def finish(t, carry):
        sl = pl.ds(pl.multiple_of(t * tile, tile), tile)
        hs = hf_ref[sl, :] + hb_ref[sl, :]
        ms = jnp.dot((hs * hs).astype(BF16), ones_bd, preferred_element_type=F32) * (1.0 / HEAD_DIM)
        hm = hs * lax.rsqrt(ms + NORM_EPS) * ng_ref[...]
        o_ref[sl, :] = (hm * _sigmoid(og_ref[sl, :].astype(F32))).astype(BF16)
        return carry

    lax.fori_loop(0, n_seq * seq // tile, finish, 0)


def _mlstm(u_ml, gates, gate_b, norm_g, batch, seq):
    n_seq = min(batch, ML_ROWS_PER_STEP // seq)
    rows = n_seq * seq
    col = lambda j: pl.BlockSpec((rows, BRANCH_WIDTH), lambda b: (b, j), pipeline_mode=pl.Buffered(1))
    return pl.pallas_call(
        functools.partial(_mlstm_body, seq=seq, n_seq=n_seq),
        grid=(batch // n_seq,),
        in_specs=[col(0), col(1), col(2), col(3),
                  pl.BlockSpec((rows, LANES), lambda b: (b, 0), pipeline_mode=pl.Buffered(1)),
                  _const_spec((1, LANES)), _const_spec((1, BRANCH_WIDTH))],
        out_specs=pl.BlockSpec((rows, BRANCH_WIDTH), lambda b: (b, 0)),
        out_shape=jax.ShapeDtypeStruct((batch * seq, BRANCH_WIDTH), BF16),
        scratch_shapes=[pltpu.VMEM((rows, BRANCH_WIDTH), F32), pltpu.VMEM((rows, BRANCH_WIDTH), F32),
                        pltpu.VMEM((2 * n_seq, BRANCH_WIDTH, BRANCH_WIDTH), F32)],
        compiler_params=_cparams(("parallel",), 56),
        name="mlstm",
    )(u_ml, u_ml, u_ml, u_ml, gates, gate_b, norm_g)


CONV_PAD = 16
CONV_TILE = 64


def _conv_body(u_ref, w_ref, b_ref, g_ref, beta_ref, o_ref, pad_ref, *, seq):
    u = u_ref[...]
    a = u[:, :BRANCH_WIDTH].astype(F32)
    gate = u[:, BRANCH_WIDTH:].astype(F32)
    zeros = jnp.zeros((CONV_PAD, BRANCH_WIDTH), F32)
    pad_ref[0:CONV_PAD, :] = zeros
    pad_ref[CONV_PAD + seq:, :] = zeros
    pad_ref[CONV_PAD:CONV_PAD + seq, :] = a * _sigmoid(gate)
    first = CONV_PAD - CONV_WIDTH // 2
    span = CONV_TILE + 2 * CONV_PAD - SUBLANES

    def tile(t, carry):
        base = pl.multiple_of(t * CONV_TILE, CONV_TILE)
        halves = []
        for c in range(BRANCH_WIDTH // LANES):
            lanes = slice(c * LANES, (c + 1) * LANES)
            win = pad_ref[pl.ds(base, CONV_TILE + 2 * CONV_PAD), lanes]
            n_blk = CONV_TILE // SUBLANES
            acc = jnp.zeros((n_blk, SUBLANES, LANES), F32)
            for r in range(SUBLANES):
                shifted = win[r:r + span, :].reshape(span // SUBLANES, SUBLANES, LANES)
                for j in range(CONV_WIDTH):
                    a8, rj = divmod(first + j, SUBLANES)
                    if rj == r:
                        acc = acc + shifted[a8:a8 + n_blk] * w_ref[j, :, lanes][None]
            halves.append(acc.reshape(CONV_TILE, LANES))
        y = jnp.concatenate(halves, axis=1) + b_ref[...]
        mu = jnp.mean(y, axis=-1, keepdims=True)
        yc = y - mu
        var = jnp.mean(yc * yc, axis=-1, keepdims=True)
        z = yc * lax.rsqrt(var + NORM_EPS) * g_ref[...] + beta_ref[...]
        o_ref[pl.ds(base, CONV_TILE), :] = (z * _sigmoid(z)).astype(BF16)
        return carry

    lax.fori_loop(0, seq // CONV_TILE, tile, 0, unroll=2)


def _conv(u_cv, w, b, g, beta, batch, seq):
    return pl.pallas_call(
        functools.partial(_conv_body, seq=seq),
        grid=(batch,),
        in_specs=[pl.BlockSpec((seq, CONV_COLS), lambda i: (i, 0)),
                  _const_spec(w.shape), _const_spec(b.shape), _const_spec(g.shape), _const_spec(beta.shape)],
        out_specs=pl.BlockSpec((seq, BRANCH_WIDTH), lambda i: (i, 0)),
        out_shape=jax.ShapeDtypeStruct((batch * seq, BRANCH_WIDTH), BF16),
        scratch_shapes=[pltpu.VMEM((seq + 2 * CONV_PAD, BRANCH_WIDTH), F32)],
        compiler_params=_cparams(("parallel",), 32),
        name="conv",
    )(u_cv, w, b, g, beta)


ATT_Q_TILE = 256


def _gqa_body(q_ref, k_ref, v_ref, o_ref):
    k = k_ref[...]
    v = v_ref[...]
    lane = lax.broadcasted_iota(jnp.int32, (ATT_Q_TILE, LANES), 1)
    low = lane < HEAD_DIM
    for c in range(2):
        qc = q_ref[:, c * LANES:(c + 1) * LANES]
        halves = []
        for keep in (low, jnp.logical_not(low)):
            qh = jnp.where(keep, qc, jnp.zeros_like(qc))
            s = lax.dot_general(qh, k, (((1,), (1,)), ((), ())), preferred_element_type=F32)
            m = jnp.max(s, axis=1, keepdims=True)
            p = jnp.exp(s - m)
            l = jnp.sum(p, axis=1, keepdims=True)
            halves.append(jnp.dot(p.astype(BF16), v, preferred_element_type=F32) / l)
        o_ref[:, c * LANES:(c + 1) * LANES] = jnp.where(low, halves[0], halves[1]).astype(BF16)


def _gqa(qk, v, batch, seq):
    tiles = seq // ATT_Q_TILE
    return pl.pallas_call(
        _gqa_body,
        grid=(batch, tiles),
        in_specs=[pl.BlockSpec((ATT_Q_TILE, N_HEADS * HEAD_DIM), lambda b, i: (b * tiles + i, 0)),
                  pl.BlockSpec((seq, LANES), lambda b, i: (b, 2)),
                  pl.BlockSpec((seq, LANES), lambda b, i: (b, 0))],
        out_specs=pl.BlockSpec((ATT_Q_TILE, BRANCH_WIDTH), lambda b, i: (b * tiles + i, 0)),
        out_shape=jax.ShapeDtypeStruct((batch * seq, BRANCH_WIDTH), BF16),
        compiler_params=_cparams(("parallel", "parallel"), 48),
        name="gqa",
    )(qk, qk, v)


def _sigmoid(x):
    return 0.5 * jnp.tanh(0.5 * x) + 0.5


def _merge_body(x_ref, g_ref, b0_ref, b1_ref, b2_ref, b3_ref, wg_ref, bg_ref, wb_ref, wo_ref, o_ref):
    x = x_ref[...]
    h = _rms(x, g_ref[...]).astype(BF16)
    acc = jnp.zeros(x.shape, F32)
    for n, br_ref in enumerate((b0_ref, b1_ref, b2_ref, b3_ref)):
        cols = slice(n * D_MODEL, (n + 1) * D_MODEL)
        gl = jnp.dot(h, wg_ref[:, cols], preferred_element_type=F32) + bg_ref[:, cols]
        pr = jnp.dot(br_ref[...], wb_ref[n], preferred_element_type=F32)
        acc = acc + _sigmoid(gl) * pr
    o_ref[...] = x + jnp.dot(acc.astype(BF16), wo_ref[...], preferred_element_type=F32)


def _merge(x, g, branches, w_gate, b_gate, w_branch, w_out):
    n_tok = x.shape[0]
    tm = 2 * TOKEN_TILE
    row = lambda i: (i, 0)
    return pl.pallas_call(
        _merge_body,
        grid=(n_tok // tm,),
        in_specs=[pl.BlockSpec((tm, D_MODEL), row), _const_spec((1, D_MODEL))]
        + [pl.BlockSpec((tm, BRANCH_WIDTH), row)] * N_BRANCHES
        + [_resident_spec(w_gate.shape), _const_spec(b_gate.shape), _resident_spec(w_branch.shape),
           _resident_spec(w_out.shape)],
        out_specs=pl.BlockSpec((tm, D_MODEL), row),
        out_shape=jax.ShapeDtypeStruct((n_tok, D_MODEL), F32),
        compiler_params=_cparams(("parallel",), 56),
        name="merge",
    )(x, g, *branches, w_gate, b_gate, w_branch, w_out)


MOE_TOKEN_TILE = 1024
MOE_CHUNK = 288
ROUTER_GROUP_LANE = ROUTER_E0 + MOE_N_EXPERTS


def _split_bf16(x):
    hi = x.astype(BF16)
    return hi, (x - hi.astype(F32)).astype(BF16)


def _router(h, wr_hi_ref, wr_lo_ref, br_ref):
    h_hi, h_lo = _split_bf16(h)
    logits = (jnp.dot(h_hi, wr_hi_ref[...], preferred_element_type=F32)
              + jnp.dot(h_lo, wr_hi_ref[...], preferred_element_type=F32)
              + jnp.dot(h_hi, wr_lo_ref[...], preferred_element_type=F32)) + br_ref[...]
    lane = lax.broadcasted_iota(jnp.int32, logits.shape, 1).astype(F32)
    big = float(ROUTER_LANES)
    is_g = lane < MOE_GROUPS
    gl = jnp.where(is_g, logits, NEG_INF)
    g_max = jnp.max(gl, axis=1, keepdims=True)
    g_idx = jnp.min(jnp.where(is_g & (gl == g_max), lane, big), axis=1, keepdims=True)
    g_gate = 1.0 / jnp.sum(jnp.where(is_g, jnp.exp(gl - g_max), 0.0), axis=1, keepdims=True)
    e_lo = ROUTER_E0 + g_idx * MOE_EXPERTS_PER_GROUP
    in_grp = (lane >= e_lo) & (lane < e_lo + MOE_EXPERTS_PER_GROUP)
    el = jnp.where(in_grp, logits, NEG_INF)
    top1 = jnp.max(el, axis=1, keepdims=True)
    idx1 = jnp.min(jnp.where(in_grp & (el == top1), lane, big), axis=1, keepdims=True)
    rest = in_grp & (lane != idx1)
    el2 = jnp.where(rest, logits, NEG_INF)
    top2 = jnp.max(el2, axis=1, keepdims=True)
    idx2 = jnp.min(jnp.where(rest & (el2 == top2), lane, big), axis=1, keepdims=True)
    e2 = jnp.exp(top2 - top1)
    w1 = g_gate / (1.0 + e2)
    w2 = g_gate * e2 / (1.0 + e2)
    rec = jnp.where(lane == idx1, w1, 0.0) + jnp.where(lane == idx2, w2, 0.0)
    return jnp.where(lane == ROUTER_GROUP_LANE, g_idx, rec)


def _moe_body(x_ref, g_ref, wrh_ref, wrl_ref, br_ref, tri_ref, w1_ref, w3_ref, w2_ref, fg_ref, o_ref,
              h_ref, acc_ref, rh_ref, rl_ref, rank_ref, cnt_ref, *, final):
    tm = MOE_TOKEN_TILE
    x = x_ref[...]
    h = _rms(x, g_ref[...])
    h_ref[...] = h.astype(BF16)
    rec = _router(h, wrh_ref, wrl_ref, br_ref)
    rec_hi, rec_lo = _split_bf16(rec)
    lane = lax.broadcasted_iota(jnp.int32, rec.shape, 1).astype(F32)
    gid = rec[:, ROUTER_GROUP_LANE:ROUTER_GROUP_LANE + 1]
    own = lane == gid
    before = jnp.dot(tri_ref[...], jnp.where(own, 1.0, 0.0).astype(BF16), preferred_element_type=F32)
    rank = jnp.sum(jnp.where(own, before, 0.0), axis=1, keepdims=True)
    totals = jnp.sum(jnp.where(own, 1.0, 0.0), axis=0, keepdims=True)
    acc_ref[...] = jnp.zeros_like(acc_ref)
    rh_ref[...] = rec_hi
    rl_ref[...] = rec_lo
    rank_ref[...] = jnp.where(lane == gid, rank, -1.0)
    for grp in range(MOE_GROUPS):
        cnt_ref[grp] = jnp.sum(jnp.where(lane[0:1, :] == grp, totals, 0.0)).astype(jnp.int32)
    contract0 = (((0,), (0,)), ((), ()))

    def one_chunk(k, grp):
        lane_t = lax.broadcasted_iota(jnp.int32, (tm, ROUTER_LANES), 1)
        rank_grp = jnp.sum(jnp.where(lane_t == grp, rank_ref[...], 0.0), axis=1, keepdims=True)
        slot = (lax.broadcasted_iota(jnp.int32, (tm, MOE_CHUNK), 1) + k * MOE_CHUNK).astype(F32)
        sel = jnp.where(rank_grp == slot, 1.0, 0.0).astype(BF16)
        xc = lax.dot_general(sel, h_ref[...], contract0, preferred_element_type=F32).astype(BF16)
        cw = (lax.dot_general(sel, rh_ref[...], contract0, preferred_element_type=F32)
              + lax.dot_general(sel, rl_ref[...], contract0, preferred_element_type=F32))
        lane_c = lax.broadcasted_iota(jnp.int32, cw.shape, 1)
        yc = jnp.zeros((MOE_CHUNK, D_MODEL), F32)
        for e in range(MOE_EXPERTS_PER_GROUP):
            ge = grp * MOE_EXPERTS_PER_GROUP + e
            a = jnp.dot(xc, w1_ref[ge], preferred_element_type=F32)
            b = jnp.dot(xc, w3_ref[ge], preferred_element_type=F32)
            c = jnp.sum(jnp.where(lane_c == ROUTER_E0 + ge, cw, 0.0), axis=1, keepdims=True)
            hid = (a * _sigmoid(a) * b * c).astype(BF16)
            yc = yc + jnp.dot(hid, w2_ref[ge], preferred_element_type=F32)
        acc_ref[...] += jnp.dot(sel, yc.astype(BF16), preferred_element_type=F32)
        return grp

    def one_group(grp, carry):
        n_chunks = (cnt_ref[grp] + MOE_CHUNK - 1) // MOE_CHUNK
        lax.fori_loop(0, n_chunks, one_chunk, grp)
        return carry

    lax.fori_loop(0, MOE_GROUPS, one_group, 0)
    y = x + acc_ref[...]
    o_ref[...] = _rms(y, fg_ref[...]) if final else y


def _moe(x, g, wr_hi, wr_lo, br, w1, w3, w2, final_g, final):
    n_tok = x.shape[0]
    tm = MOE_TOKEN_TILE
    row = lambda i: (i, 0)
    once = lambda shape: pl.BlockSpec(shape, lambda i: (0,) * len(shape), pipeline_mode=pl.Buffered(1))
    earlier = np.tril(np.ones((tm, tm), np.float32), -1)
    return pl.pallas_call(
        functools.partial(_moe_body, final=final),
        grid=(n_tok // tm,),
        in_specs=[pl.BlockSpec((tm, D_MODEL), row), _const_spec((1, D_MODEL)),
                  _const_spec(wr_hi.shape), _const_spec(wr_lo.shape), _const_spec(br.shape), once((tm, tm)),
                  once(w1.shape), once(w3.shape), once(w2.shape), _const_spec((1, D_MODEL))],
        out_specs=pl.BlockSpec((tm, D_MODEL), row),
        out_shape=jax.ShapeDtypeStruct((n_tok, D_MODEL), F32),
        scratch_shapes=[pltpu.VMEM((tm, D_MODEL), BF16), pltpu.VMEM((tm, D_MODEL), F32),
                        pltpu.VMEM((tm, ROUTER_LANES), BF16), pltpu.VMEM((tm, ROUTER_LANES), BF16),
                        pltpu.VMEM((tm, ROUTER_LANES), F32), pltpu.SMEM((MOE_GROUPS,), jnp.int32)],
        compiler_params=_cparams(("parallel",), 60),
        name="moe",
    )(x, g, wr_hi, wr_lo, br, jnp.asarray(earlier, BF16), w1, w3, w2, final_g)


def _rope_tables(seq):
    t = jnp.arange(seq)
    pos = jnp.stack([t // GRID_W, t % GRID_W], axis=-1).astype(F32)
    inv = ROPE_THETA ** (-jnp.arange(0, ROPE_AXIS_DIM, 2, dtype=F32) / ROPE_AXIS_DIM)
    ang = pos[..., None] * inv
    ang = jnp.concatenate([ang, ang], axis=-1).reshape(seq, HEAD_DIM)
    first = (jnp.arange(HEAD_DIM) % ROPE_AXIS_DIM) < (ROPE_AXIS_DIM // 2)
    cos = jnp.cos(ang)
    sin = jnp.where(first[None, :], -jnp.sin(ang), jnp.sin(ang))
    reps = QK_COLS // HEAD_DIM
    return jnp.tile(cos, (1, reps)), jnp.tile(sin, (1, reps))


ATT_HEAD_ORDER = (0, 2, 1, 3)


def _pack_layer(l, w_in, na_rpb, ml_gate_b, ml_norm_g, conv_dw_w, conv_dw_b, conv_norm_g, conv_norm_b,
                att_q_norm_g, att_k_norm_g, w_branch, w_gate, b_gate, w_out, moe_w_group, moe_b_group,
                moe_w_expert, moe_b_expert, moe_w1, moe_w3, moe_w2):
    scale = HEAD_DIM ** -0.5
    w = w_in[l]
    o0 = 0
    wna = w[:, o0:o0 + NA_COLS]
    wna = jnp.concatenate([wna[:, :BRANCH_WIDTH] * scale, wna[:, BRANCH_WIDTH:]], axis=1)
    o0 += NA_COLS
    wml = w[:, o0:o0 + 4 * BRANCH_WIDTH]
    wml = jnp.concatenate([wml[:, :BRANCH_WIDTH], wml[:, BRANCH_WIDTH:2 * BRANCH_WIDTH] * scale,
                           wml[:, 2 * BRANCH_WIDTH:]], axis=1)
    o0 += 4 * BRANCH_WIDTH
    n_gates = ML_N_GATES * N_HEADS
    wg = jnp.pad(w[:, o0:o0 + n_gates], ((0, 0), (0, LANES - n_gates)))
    o0 += n_gates
    wcv = w[:, o0:o0 + CONV_COLS]
    o0 += CONV_COLS
    wat = w[:, o0:o0 + ATT_COLS]
    wq = wat[:, :N_HEADS * HEAD_DIM].reshape(D_MODEL, N_HEADS, HEAD_DIM)[:, ATT_HEAD_ORDER, :]
    wat = jnp.concatenate([wq.reshape(D_MODEL, N_HEADS * HEAD_DIM), wat[:, N_HEADS * HEAD_DIM:]], axis=1)
    qkg = jnp.concatenate([jnp.tile(att_q_norm_g[l] * scale, N_HEADS), jnp.tile(att_k_norm_g[l], ATT_KV_HEADS)])
    wb = w_branch[l]
    wb3 = wb[3].reshape(N_HEADS, HEAD_DIM, D_MODEL)[ATT_HEAD_ORDER, :, :].reshape(BRANCH_WIDTH, D_MODEL)
    wb = jnp.concatenate([wb[:3], wb3[None]], axis=0)
    wr = jnp.concatenate([moe_w_group[l], moe_w_expert[l].reshape(D_MODEL, MOE_N_EXPERTS)], axis=1)
    n_r = MOE_GROUPS + MOE_N_EXPERTS
    br = jnp.concatenate([moe_b_group[l], moe_b_expert[l].reshape(MOE_N_EXPERTS)])
    wr_hi, wr_lo = _split_bf16(jnp.pad(wr, ((0, 0), (0, ROUTER_LANES - n_r))))
    return dict(
        wna=wna.astype(BF16), wml=wml.astype(BF16), wg=wg.astype(BF16), wcv=wcv.astype(BF16), wat=wat.astype(BF16),
        qkg=qkg.reshape(1, QK_COLS).astype(F32),
        na_bias=_na_bias_table(na_rpb[l]),
        gate_b=jnp.pad(ml_gate_b[l].reshape(1, n_gates), ((0, 0), (0, LANES - n_gates))).astype(F32),
        ml_norm_g=ml_norm_g[l].reshape(1, BRANCH_WIDTH),
        conv_w=jnp.broadcast_to(conv_dw_w[l][:, None, :], (CONV_WIDTH, SUBLANES, BRANCH_WIDTH)), conv_b=conv_dw_b[l].reshape(1, BRANCH_WIDTH),
        conv_g=conv_norm_g[l].reshape(1, BRANCH_WIDTH), conv_beta=conv_norm_b[l].reshape(1, BRANCH_WIDTH),
        w_gate=w_gate[l].astype(BF16), b_gate=b_gate[l].reshape(1, N_BRANCHES * D_MODEL),
        w_branch=wb.astype(BF16), w_out=w_out[l].astype(BF16),
        wr_hi=wr_hi, wr_lo=wr_lo, br=jnp.pad(br, (0, ROUTER_LANES - n_r)).reshape(1, ROUTER_LANES),
        w1=moe_w1[l].astype(BF16), w3=moe_w3[l].astype(BF16), w2=moe_w2[l].astype(BF16),
    )


def _trunk(x, layers, norm_mix_g, norm_ffn_g, final_norm_g, ones_qk, depth):
    batch, seq, _ = x.shape
    xf = x.reshape(batch * seq, D_MODEL)
    cos_t, sin_t = _rope_tables(seq)
    fg = final_norm_g.reshape(1, D_MODEL)
    for l in range(depth):
        p = layers[l]
        gm = norm_mix_g[l].reshape(1, D_MODEL)
        u_na, u_ml, gates, u_cv, qk, v = _inproj(xf, gm, p["wna"], p["wml"], p["wg"], p["wcv"], p["wat"], ones_qk,
                                                 p["qkg"], cos_t, sin_t, seq)
        o_na = _na(u_na, p["na_bias"], batch, seq)
        o_ml = _mlstm(u_ml, gates, p["gate_b"], p["ml_norm_g"], batch, seq)
        o_cv = _conv(u_cv, p["conv_w"], p["conv_b"], p["conv_g"], p["conv_beta"], batch, seq)
        o_at = _gqa(qk, v, batch, seq)
        xf = _merge(xf, gm, (o_na, o_ml, o_cv, o_at), p["w_gate"], p["b_gate"], p["w_branch"], p["w_out"])
        xf = _moe(xf, norm_ffn_g[l].reshape(1, D_MODEL), p["wr_hi"], p["wr_lo"], p["br"], p["w1"], p["w3"], p["w2"], fg,
                  final=(l == depth - 1))
    return xf.reshape(batch, seq, D_MODEL)


def kernel(x_prompt, x_sample, norm_mix_g, w_in, na_rpb, ml_gate_b, ml_norm_g, conv_dw_w, conv_dw_b, conv_norm_g, conv_norm_b, att_q_norm_g, att_k_norm_g, w_branch, w_gate, b_gate, w_out, norm_ffn_g, moe_w_group, moe_b_group, moe_w_expert, moe_b_expert, moe_w1, moe_w3, moe_w2, final_norm_g):
    depth = w_in.shape[0]
    layers = [_pack_layer(l, w_in, na_rpb, ml_gate_b, ml_norm_g, conv_dw_w, conv_dw_b, conv_norm_g, conv_norm_b,
                          att_q_norm_g, att_k_norm_g, w_branch, w_gate, b_gate, w_out, moe_w_group, moe_b_group,
                          moe_w_expert, moe_b_expert, moe_w1, moe_w3, moe_w2) for l in range(depth)]
    hid = np.arange(QK_COLS) // HEAD_DIM
    ones_qk = jnp.asarray(hid[:, None] == hid[None, :], BF16)
    y_prompt = _trunk(x_prompt, layers, norm_mix_g, norm_ffn_g, final_norm_g, ones_qk, depth)
    y_sample = _trunk(x_sample, layers, norm_mix_g, norm_ffn_g, final_norm_g, ones_qk, depth)
    return (y_prompt, y_sample)
```

```python
import functools

import jax
import jax.numpy as jnp
import numpy as np
from jax import lax
from jax.experimental import pallas as pl
from jax.experimental.pallas import tpu as pltpu

F32 = jnp.float32
BF16 = jnp.bfloat16

D_MODEL = 1024
GRID_W = 64
HEAD_DIM = 64
BRANCH_WIDTH = 256
N_BRANCHES = 4
N_HEADS = 4
NA_KH = 8
NA_KW = 16
ML_CHUNK = 64
ML_N_GATES = 4
CONV_WIDTH = 31
ATT_KV_HEADS = 2
ROPE_THETA = 10000.0
ROPE_AXIS_DIM = HEAD_DIM // 2
MOE_GROUPS = 4
MOE_EXPERTS_PER_GROUP = 4
MOE_N_EXPERTS = 16
MOE_D_EXPERT = 256
NORM_EPS = 1e-6
NEG_INF = -1e30

NA_COLS = 3 * BRANCH_WIDTH
ML_COLS = 4 * BRANCH_WIDTH + ML_N_GATES * N_HEADS
CONV_COLS = 2 * BRANCH_WIDTH
ATT_COLS = (N_HEADS + 2 * ATT_KV_HEADS) * HEAD_DIM
QK_COLS = (N_HEADS + ATT_KV_HEADS) * HEAD_DIM

LANES = 128
SUBLANES = 8
ROUTER_LANES = LANES
ROUTER_E0 = MOE_GROUPS
V7X_VMEM_BYTES = 64 * 1024 * 1024

TOKEN_TILE = 512


def _cparams(semantics, vmem_mb):
    assert vmem_mb * 1024 * 1024 < V7X_VMEM_BYTES
    return pltpu.CompilerParams(dimension_semantics=semantics, vmem_limit_bytes=vmem_mb * 1024 * 1024)


def _const_spec(shape):
    nd = len(shape)
    return pl.BlockSpec(shape, lambda *_: (0,) * nd)


def _resident_spec(shape):
    nd = len(shape)
    return pl.BlockSpec(shape, lambda *_: (0,) * nd, pipeline_mode=pl.Buffered(1))


def _rms(x, g):
    ms = jnp.mean(x * x, axis=-1, keepdims=True)
    return x * lax.rsqrt(ms + NORM_EPS) * g


def _inproj_body(x_ref, g_ref, wna_ref, wml_ref, wg_ref, wcv_ref, wat_ref, ones_ref, qkg_ref, cos_ref, sin_ref,
                 una_ref, uml_ref, gates_ref, ucv_ref, qk_ref, v_ref):
    h = _rms(x_ref[...], g_ref[...]).astype(BF16)
    una_ref[...] = jnp.dot(h, wna_ref[...], preferred_element_type=F32).astype(BF16)
    uml_ref[...] = jnp.dot(h, wml_ref[...], preferred_element_type=F32).astype(BF16)
    gates_ref[...] = jnp.dot(h, wg_ref[...], preferred_element_type=F32)
    ucv_ref[...] = jnp.dot(h, wcv_ref[...], preferred_element_type=F32).astype(BF16)
    ua = jnp.dot(h, wat_ref[...], preferred_element_type=F32)
    qk = ua[:, :QK_COLS]
    ms = jnp.dot((qk * qk).astype(BF16), ones_ref[...], preferred_element_type=F32) * (1.0 / HEAD_DIM)
    qn = qk * lax.rsqrt(ms + NORM_EPS) * qkg_ref[...]
    half = ROPE_AXIS_DIM // 2
    rot = []
    for c in range(QK_COLS // LANES):
        xc = qn[:, c * LANES:(c + 1) * LANES]
        lane = lax.broadcasted_iota(jnp.int32, xc.shape, 1)
        rot.append(jnp.where((lane % ROPE_AXIS_DIM) < half, pltpu.roll(xc, LANES - half, 1), pltpu.roll(xc, half, 1)))
    rot = jnp.concatenate(rot, axis=1)
    qk_ref[...] = (qn * cos_ref[...] + rot * sin_ref[...]).astype(BF16)
    v_ref[...] = ua[:, QK_COLS:].astype(BF16)


def _inproj(x, g, wna, wml, wg, wcv, wat, ones_qk, qkg, cos_t, sin_t, seq):
    n_tok = x.shape[0]
    tm = 2 * TOKEN_TILE
    tiles_per_seq = seq // tm
    row = lambda i: (i, 0)
    pos = lambda i: (i % tiles_per_seq, 0)
    outs = [
        jax.ShapeDtypeStruct((n_tok, NA_COLS), BF16),
        jax.ShapeDtypeStruct((n_tok, 4 * BRANCH_WIDTH), BF16),
        jax.ShapeDtypeStruct((n_tok, LANES), F32),
        jax.ShapeDtypeStruct((n_tok, CONV_COLS), BF16),
        jax.ShapeDtypeStruct((n_tok, QK_COLS), BF16),
        jax.ShapeDtypeStruct((n_tok, ATT_KV_HEADS * HEAD_DIM), BF16),
    ]
    return pl.pallas_call(
        _inproj_body,
        grid=(n_tok // tm,),
        in_specs=[
            pl.BlockSpec((tm, D_MODEL), row),
            _const_spec((1, D_MODEL)),
            _resident_spec(wna.shape), _resident_spec(wml.shape), _const_spec(wg.shape), _resident_spec(wcv.shape),
            _resident_spec(wat.shape), _const_spec(ones_qk.shape), _const_spec(qkg.shape),
            pl.BlockSpec((tm, QK_COLS), pos), pl.BlockSpec((tm, QK_COLS), pos),
        ],
        out_specs=[pl.BlockSpec((tm, o.shape[1]), row) for o in outs],
        out_shape=outs,
        compiler_params=_cparams(("parallel",), 48),
        name="inproj",
    )(x, g, wna, wml, wg, wcv, wat, ones_qk, qkg, cos_t, sin_t)


def _block_diag_mask(rows, cols):
    r = lax.broadcasted_iota(jnp.int32, (rows, cols), 0) // HEAD_DIM
    c = lax.broadcasted_iota(jnp.int32, (rows, cols), 1) // HEAD_DIM
    return r == (c % N_HEADS)


def _tile_heads(x):
    t = jnp.concatenate([x] * N_HEADS, axis=0)
    return jnp.where(_block_diag_mask(N_HEADS * HEAD_DIM, BRANCH_WIDTH), t, jnp.zeros_like(t))


def _na_body(q_ref, k_ref, v_ref, bias_ref, o_ref, *, rows):
    n_keys = NA_KH * GRID_W

    def one_row(r, carry):
        r0 = jnp.clip(r - NA_KH // 2, 0, rows - NA_KH)
        delta = r0 - r + NA_KH - 1
        q = q_ref[pl.ds(pl.multiple_of(r * GRID_W, GRID_W), GRID_W), :]
        kk = k_ref[pl.ds(pl.multiple_of(r0 * GRID_W, GRID_W), n_keys), :]
        vv = v_ref[pl.ds(pl.multiple_of(r0 * GRID_W, GRID_W), n_keys), :]
        s = lax.dot_general(_tile_heads(q), kk, (((1,), (1,)), ((), ())), preferred_element_type=F32)
        s = s + bias_ref[delta]
        m = jnp.max(s, axis=1, keepdims=True)
        p = jnp.exp(s - m)
        l = jnp.sum(p, axis=1, keepdims=True)
        of = jnp.dot(p.astype(BF16), vv, preferred_element_type=F32) / l
        of = jnp.where(_block_diag_mask(N_HEADS * HEAD_DIM, BRANCH_WIDTH), of, 0.0)
        o = of[0:HEAD_DIM] + of[HEAD_DIM:2 * HEAD_DIM] + of[2 * HEAD_DIM:3 * HEAD_DIM] + of[3 * HEAD_DIM:]
        o_ref[pl.ds(pl.multiple_of(r * GRID_W, GRID_W), GRID_W), :] = o.astype(BF16)
        return carry

    lax.fori_loop(0, rows, one_row, 0, unroll=8)


def _na_bias_table(rpb):
    qc = np.arange(GRID_W)[:, None]
    kc = np.arange(GRID_W)[None, :]
    win = np.clip(qc - NA_KW // 2, 0, GRID_W - NA_KW)
    in_win = (kc >= win) & (kc < win + NA_KW)
    col_off = np.clip(kc - qc + NA_KW - 1, 0, 2 * NA_KW - 2)
    n_ro, n_co = 2 * NA_KH - 1, 2 * NA_KW - 1
    pick = np.zeros((n_co, GRID_W * GRID_W), np.float32)
    pick[col_off.reshape(-1), np.arange(GRID_W * GRID_W)] = 1.0
    c = jnp.dot(rpb.astype(F32).reshape(N_HEADS * n_ro, n_co), pick, precision=lax.Precision.HIGHEST)
    c = jnp.where(in_win.reshape(1, -1), c, NEG_INF).reshape(N_HEADS, n_ro, GRID_W, GRID_W)
    c = c.transpose(0, 2, 1, 3)
    b = jnp.stack([c[:, :, d:d + NA_KH, :] for d in range(NA_KH)], axis=0)
    return b.reshape(NA_KH, N_HEADS * GRID_W, NA_KH * GRID_W)


def _na(u_na, bias, batch, seq):
    rows = seq // GRID_W
    return pl.pallas_call(
        functools.partial(_na_body, rows=rows),
        grid=(batch,),
        in_specs=[
            pl.BlockSpec((seq, BRANCH_WIDTH), lambda b: (b, 0)),
            pl.BlockSpec((seq, BRANCH_WIDTH), lambda b: (b, 1)),
            pl.BlockSpec((seq, BRANCH_WIDTH), lambda b: (b, 2)),
            _const_spec(bias.shape),
        ],
        out_specs=pl.BlockSpec((seq, BRANCH_WIDTH), lambda b: (b, 0)),
        out_shape=jax.ShapeDtypeStruct((batch * seq, BRANCH_WIDTH), BF16),
        compiler_params=_cparams(("parallel",), 48),
        name="na",
    )(u_na, u_na, u_na, bias)


def _expand_heads(cols):
    hid = lax.broadcasted_iota(jnp.int32, (ML_CHUNK, BRANCH_WIDTH), 1) // HEAD_DIM
    out = jnp.broadcast_to(cols[:, 0:1], (ML_CHUNK, BRANCH_WIDTH))
    for h in range(1, N_HEADS):
        out = jnp.where(hid == h, cols[:, h:h + 1], out)
    return out


def _head_rowmax(x):
    hid = lax.broadcasted_iota(jnp.int32, x.shape, 1) // HEAD_DIM
    out = jnp.zeros_like(x)
    for h in range(N_HEADS):
        mh = jnp.max(jnp.where(hid == h, x, NEG_INF), axis=1, keepdims=True)
        out = jnp.where(hid == h, mh, out)
    return out


def _mlstm_step_fn(reverse, row0, q_ref, k_ref, v_ref, gates_ref, gb_ref, h_ref, state_ref, n_chunks):
    L = ML_CHUNK
    W = BRANCH_WIDTH
    sub = lax.broadcasted_iota(jnp.int32, (L, W), 0)
    pos = lax.broadcasted_iota(jnp.int32, (L, W), 1) % HEAD_DIM
    diag = sub == pos
    causal = (pos >= sub) if reverse else (pos <= sub)
    tr = lax.broadcasted_iota(jnp.int32, (L, L), 0)
    tc = lax.broadcasted_iota(jnp.int32, (L, L), 1)
    tri = jnp.where((tc >= tr) if reverse else (tc <= tr), 1.0, 0.0).astype(BF16)
    ones_bd = _block_diag_mask(W, W).astype(BF16)
    bd = _block_diag_mask(W, W)
    i_off, f_off = (2 * N_HEADS, 3 * N_HEADS) if reverse else (0, N_HEADS)
    state_ref[...] = jnp.zeros_like(state_ref)

    def step(ci, carry):
        m_run, n_run = carry
        c = (n_chunks - 1 - ci) if reverse else ci
        sl = pl.ds(pl.multiple_of(row0 + c * L, L), L)
        q = q_ref[sl, :]
        k = k_ref[sl, :]
        v = v_ref[sl, :]
        g = gates_ref[sl, :] + gb_ref[...]
        i_e = _expand_heads(g[:, i_off:i_off + N_HEADS])
        f_e = _expand_heads(jax.nn.log_sigmoid(g[:, f_off:f_off + N_HEADS]))
        f_hi = f_e.astype(BF16)
        f_r1 = f_e - f_hi.astype(F32)
        f_mid = f_r1.astype(BF16)
        f_lo = (f_r1 - f_mid.astype(F32)).astype(BF16)
        b = (jnp.dot(tri, f_hi, preferred_element_type=F32) + jnp.dot(tri, f_mid, preferred_element_type=F32)
             + jnp.dot(tri, f_lo, preferred_element_type=F32))
        b_end = b[0:1, :] if reverse else b[L - 1:L, :]
        rrow = jnp.sum(jnp.where(diag, i_e - b, 0.0), axis=0, keepdims=True)
        log_d = jnp.where(causal, b + rrow, NEG_INF)
        inter = b + m_run
        m_row = jnp.maximum(inter, _head_rowmax(log_d))
        d = jnp.exp(log_d - m_row)
        w_inter = jnp.exp(inter - m_row)
        s = lax.dot_general(q, _tile_heads(k), (((1,), (1,)), ((), ())), preferred_element_type=F32)
        p = (s * d).astype(BF16)
        r_loc = jnp.dot(p, jnp.concatenate([_tile_heads(v), ones_bd], axis=1), preferred_element_type=F32)
        num_int = jnp.dot(q, state_ref[...].astype(BF16), preferred_element_type=F32)
        den_int = jnp.dot((q.astype(F32) * n_run).astype(BF16), ones_bd, preferred_element_type=F32)
        num = r_loc[:, :W] + w_inter * num_int
        den = r_loc[:, W:] + w_inter * den_int
        h_ref[sl, :] = num / jnp.maximum(jnp.abs(den), jnp.exp(-m_row))
        log_w = b_end - b + i_e
        m_new = jnp.maximum(b_end + m_run, jnp.max(log_w, axis=0, keepdims=True))
        w = jnp.exp(log_w - m_new)
        decay = jnp.exp(b_end + m_run - m_new)
        upd = lax.dot_general(k, (w * v.astype(F32)).astype(BF16), (((0,), (0,)), ((), ())),
                              preferred_element_type=F32)
        state_ref[...] = decay * state_ref[...] + jnp.where(bd, upd, 0.0)
        n_new = decay * n_run + jnp.sum(w * k.astype(F32), axis=0, keepdims=True)
        return m_new, n_new

    return step


ML_ROWS_PER_STEP = 8192


def _mlstm_body(q_ref, k_ref, v_ref, og_ref, gates_ref, gb_ref, ng_ref, o_ref, hf_ref, hb_ref, state_ref, *, seq, n_seq):
    n_chunks = seq // ML_CHUNK
    steps = []
    for j in range(n_seq):
        for reverse in (False, True):
            steps.append(_mlstm_step_fn(reverse, j * seq, q_ref, k_ref, v_ref, gates_ref, gb_ref,
                                        hb_ref if reverse else hf_ref, state_ref.at[2 * j + int(reverse)], n_chunks))
    start = (jnp.full((1, BRANCH_WIDTH), NEG_INF, F32), jnp.zeros((1, BRANCH_WIDTH), F32))
    lax.fori_loop(0, n_chunks, lambda ci, cs: tuple(step(ci, c) for step, c in zip(steps, cs)), (start,) * len(steps))
    ones_bd = _block_diag_mask(BRANCH_WIDTH, BRANCH_WIDTH).astype(BF16)
    tile = 256

    def finish(t, carry):
        sl = pl.ds(pl.multiple_of(t * tile, tile), tile)
        hs = hf_ref[sl, :] + hb_ref[sl, :]
        ms = jnp.dot((hs * hs).astype(BF16), ones_bd, preferred_element_type=F32) * (1.0 / HEAD_DIM)
        hm = hs * lax.rsqrt(ms + NORM_EPS) * ng_ref[...]
        o_ref[sl, :] = (hm * _sigmoid(og_ref[sl, :].astype(F32))).astype(BF16)
        return carry

    lax.fori_loop(0, n_seq * seq // tile, finish, 0)


def _mlstm(u_ml, gates, gate_b, norm_g, batch, seq):
    n_seq = min(batch, ML_ROWS_PER_STEP // seq)
    rows = n_seq * seq
    col = lambda j: pl.BlockSpec((rows, BRANCH_WIDTH), lambda b: (b, j), pipeline_mode=pl.Buffered(1))
    return pl.pallas_call(
        functools.partial(_mlstm_body, seq=seq, n_seq=n_seq),
        grid=(batch // n_seq,),
        in_specs=[col(0), col(1), col(2), col(3),
                  pl.BlockSpec((rows, LANES), lambda b: (b, 0), pipeline_mode=pl.Buffered(1)),
                  _const_spec((1, LANES)), _const_spec((1, BRANCH_WIDTH))],
        out_specs=pl.BlockSpec((rows, BRANCH_WIDTH), lambda b: (b, 0)),
        out_shape=jax.ShapeDtypeStruct((batch * seq, BRANCH_WIDTH), BF16),
        scratch_shapes=[pltpu.VMEM((rows, BRANCH_WIDTH), F32), pltpu.VMEM((rows, BRANCH_WIDTH), F32),
                        pltpu.VMEM((2 * n_seq, BRANCH_WIDTH, BRANCH_WIDTH), F32)],
        compiler_params=_cparams(("parallel",), 56),
        name="mlstm",
    )(u_ml, u_ml, u_ml, u_ml, gates, gate_b, norm_g)


CONV_PAD = 16
CONV_TILE = 64


def _conv_body(u_ref, w_ref, b_ref, g_ref, beta_ref, o_ref, pad_ref, *, seq):
    u = u_ref[...]
    a = u[:, :BRANCH_WIDTH].astype(F32)
    gate = u[:, BRANCH_WIDTH:].astype(F32)
    zeros = jnp.zeros((CONV_PAD, BRANCH_WIDTH), F32)
    pad_ref[0:CONV_PAD, :] = zeros
    pad_ref[CONV_PAD + seq:, :] = zeros
    pad_ref[CONV_PAD:CONV_PAD + seq, :] = a * _sigmoid(gate)
    first = CONV_PAD - CONV_WIDTH // 2
    span = CONV_TILE + 2 * CONV_PAD - SUBLANES

    def tile(t, carry):
        base = pl.multiple_of(t * CONV_TILE, CONV_TILE)
        halves = []
        for c in range(BRANCH_WIDTH // LANES):
            lanes = slice(c * LANES, (c + 1) * LANES)
            win = pad_ref[pl.ds(base, CONV_TILE + 2 * CONV_PAD), lanes]
            n_blk = CONV_TILE // SUBLANES
            acc = jnp.zeros((n_blk, SUBLANES, LANES), F32)
            for r in range(SUBLANES):
                shifted = win[r:r + span, :].reshape(span // SUBLANES, SUBLANES, LANES)
                for j in range(CONV_WIDTH):
                    a8, rj = divmod(first + j, SUBLANES)
                    if rj == r:
                        acc = acc + shifted[a8:a8 + n_blk] * w_ref[j, :, lanes][None]
            halves.append(acc.reshape(CONV_TILE, LANES))
        y = jnp.concatenate(halves, axis=1) + b_ref[...]
        mu = jnp.mean(y, axis=-1, keepdims=True)
        yc = y - mu
        var = jnp.mean(yc * yc, axis=-1, keepdims=True)
        z = yc * lax.rsqrt(var + NORM_EPS) * g_ref[...] + beta_ref[...]
        o_ref[pl.ds(base, CONV_TILE), :] = (z * _sigmoid(z)).astype(BF16)
        return carry

    lax.fori_loop(0, seq // CONV_TILE, tile, 0, unroll=4)


def _conv(u_cv, w, b, g, beta, batch, seq):
    return pl.pallas_call(
        functools.partial(_conv_body, seq=seq),
        grid=(batch,),
        in_specs=[pl.BlockSpec((seq, CONV_COLS), lambda i: (i, 0)),
                  _const_spec(w.shape), _const_spec(b.shape), _const_spec(g.shape), _const_spec(beta.shape)],
        out_specs=pl.BlockSpec((seq, BRANCH_WIDTH), lambda i: (i, 0)),
        out_shape=jax.ShapeDtypeStruct((batch * seq, BRANCH_WIDTH), BF16),
        scratch_shapes=[pltpu.VMEM((seq + 2 * CONV_PAD, BRANCH_WIDTH), F32)],
        compiler_params=_cparams(("parallel",), 32),
        name="conv",
    )(u_cv, w, b, g, beta)


ATT_Q_TILE = 256


def _gqa_body(q_ref, k_ref, v_ref, o_ref):
    k = k_ref[...]
    v = v_ref[...]
    lane = lax.broadcasted_iota(jnp.int32, (ATT_Q_TILE, LANES), 1)
    low = lane < HEAD_DIM
    for c in range(2):
        qc = q_ref[:, c * LANES:(c + 1) * LANES]
        halves = []
        for keep in (low, jnp.logical_not(low)):
            qh = jnp.where(keep, qc, jnp.zeros_like(qc))
            s = lax.dot_general(qh, k, (((1,), (1,)), ((), ())), preferred_element_type=F32)
            m = jnp.max(s, axis=1, keepdims=True)
            p = jnp.exp(s - m)
            l = jnp.sum(p, axis=1, keepdims=True)
            halves.append(jnp.dot(p.astype(BF16), v, preferred_element_type=F32) / l)
        o_ref[:, c * LANES:(c + 1) * LANES] = jnp.where(low, halves[0], halves[1]).astype(BF16)


def _gqa(qk, v, batch, seq):
    tiles = seq // ATT_Q_TILE
    return pl.pallas_call(
        _gqa_body,
        grid=(batch, tiles),
        in_specs=[pl.BlockSpec((ATT_Q_TILE, N_HEADS * HEAD_DIM), lambda b, i: (b * tiles + i, 0)),
                  pl.BlockSpec((seq, LANES), lambda b, i: (b, 2)),
                  pl.BlockSpec((seq, LANES), lambda b, i: (b, 0))],
        out_specs=pl.BlockSpec((ATT_Q_TILE, BRANCH_WIDTH), lambda b, i: (b * tiles + i, 0)),
        out_shape=jax.ShapeDtypeStruct((batch * seq, BRANCH_WIDTH), BF16),
        compiler_params=_cparams(("parallel", "parallel"), 48),
        name="gqa",
    )(qk, qk, v)


def _sigmoid(x):
    return 0.5 * jnp.tanh(0.5 * x) + 0.5


def _merge_body(x_ref, g_ref, b0_ref, b1_ref, b2_ref, b3_ref, wg_ref, bg_ref, wb_ref, wo_ref, o_ref):
    x = x_ref[...]
    h = _rms(x, g_ref[...]).astype(BF16)
    acc = jnp.zeros(x.shape, F32)
    for n, br_ref in enumerate((b0_ref, b1_ref, b2_ref, b3_ref)):
        cols = slice(n * D_MODEL, (n + 1) * D_MODEL)
        gl = jnp.dot(h, wg_ref[:, cols], preferred_element_type=F32) + bg_ref[:, cols]
        pr = jnp.dot(br_ref[...], wb_ref[n], preferred_element_type=F32)
        acc = acc + _sigmoid(gl) * pr
    o_ref[...] = x + jnp.dot(acc.astype(BF16), wo_ref[...], preferred_element_type=F32)


def _merge(x, g, branches, w_gate, b_gate, w_branch, w_out):
    n_tok = x.shape[0]
    tm = 2 * TOKEN_TILE
    row = lambda i: (i, 0)
    return pl.pallas_call(
        _merge_body,
        grid=(n_tok // tm,),
        in_specs=[pl.BlockSpec((tm, D_MODEL), row), _const_spec((1, D_MODEL))]
        + [pl.BlockSpec((tm, BRANCH_WIDTH), row)] * N_BRANCHES
        + [_resident_spec(w_gate.shape), _const_spec(b_gate.shape), _resident_spec(w_branch.shape),
           _resident_spec(w_out.shape)],
        out_specs=pl.BlockSpec((tm, D_MODEL), row),
        out_shape=jax.ShapeDtypeStruct((n_tok, D_MODEL), F32),
        compiler_params=_cparams(("parallel",), 56),
        name="merge",
    )(x, g, *branches, w_gate, b_gate, w_branch, w_out)


MOE_TOKEN_TILE = 1024
MOE_CHUNK = 320
ROUTER_GROUP_LANE = ROUTER_E0 + MOE_N_EXPERTS


def _split_bf16(x):
    hi = x.astype(BF16)
    return hi, (x - hi.astype(F32)).astype(BF16)


def _router(h, wr_hi_ref, wr_lo_ref, br_ref):
    h_hi, h_lo = _split_bf16(h)
    logits = (jnp.dot(h_hi, wr_hi_ref[...], preferred_element_type=F32)
              + jnp.dot(h_lo, wr_hi_ref[...], preferred_element_type=F32)
              + jnp.dot(h_hi, wr_lo_ref[...], preferred_element_type=F32)) + br_ref[...]
    lane = lax.broadcasted_iota(jnp.int32, logits.shape, 1).astype(F32)
    big = float(ROUTER_LANES)
    is_g = lane < MOE_GROUPS
    gl = jnp.where(is_g, logits, NEG_INF)
    g_max = jnp.max(gl, axis=1, keepdims=True)
    g_idx = jnp.min(jnp.where(is_g & (gl == g_max), lane, big), axis=1, keepdims=True)
    g_gate = 1.0 / jnp.sum(jnp.where(is_g, jnp.exp(gl - g_max), 0.0), axis=1, keepdims=True)
    e_lo = ROUTER_E0 + g_idx * MOE_EXPERTS_PER_GROUP
    in_grp = (lane >= e_lo) & (lane < e_lo + MOE_EXPERTS_PER_GROUP)
    el = jnp.where(in_grp, logits, NEG_INF)
    top1 = jnp.max(el, axis=1, keepdims=True)
    idx1 = jnp.min(jnp.where(in_grp & (el == top1), lane, big), axis=1, keepdims=True)
    rest = in_grp & (lane != idx1)
    el2 = jnp.where(rest, logits, NEG_INF)
    top2 = jnp.max(el2, axis=1, keepdims=True)
    idx2 = jnp.min(jnp.where(rest & (el2 == top2), lane, big), axis=1, keepdims=True)
    e2 = jnp.exp(top2 - top1)
    w1 = g_gate / (1.0 + e2)
    w2 = g_gate * e2 / (1.0 + e2)
    rec = jnp.where(lane == idx1, w1, 0.0) + jnp.where(lane == idx2, w2, 0.0)
    return jnp.where(lane == ROUTER_GROUP_LANE, g_idx, rec)


def _moe_body(x_ref, g_ref, wrh_ref, wrl_ref, br_ref, tri_ref, w1_ref, w3_ref, w2_ref, fg_ref, o_ref,
              h_ref, acc_ref, rh_ref, rl_ref, rank_ref, cnt_ref, *, final):
    tm = MOE_TOKEN_TILE
    x = x_ref[...]
    h = _rms(x, g_ref[...])
    h_ref[...] = h.astype(BF16)
    rec = _router(h, wrh_ref, wrl_ref, br_ref)
    rec_hi, rec_lo = _split_bf16(rec)
    lane = lax.broadcasted_iota(jnp.int32, rec.shape, 1).astype(F32)
    gid = rec[:, ROUTER_GROUP_LANE:ROUTER_GROUP_LANE + 1]
    own = lane == gid
    before = jnp.dot(tri_ref[...], jnp.where(own, 1.0, 0.0).astype(BF16), preferred_element_type=F32)
    rank = jnp.sum(jnp.where(own, before, 0.0), axis=1, keepdims=True)
    totals = jnp.sum(jnp.where(own, 1.0, 0.0), axis=0, keepdims=True)
    acc_ref[...] = jnp.zeros_like(acc_ref)
    rh_ref[...] = rec_hi
    rl_ref[...] = rec_lo
    rank_ref[...] = jnp.where(lane == gid, rank, -1.0)
    for grp in range(MOE_GROUPS):
        cnt_ref[grp] = jnp.sum(jnp.where(lane[0:1, :] == grp, totals, 0.0)).astype(jnp.int32)
    contract0 = (((0,), (0,)), ((), ()))

    def one_chunk(k, grp):
        lane_t = lax.broadcasted_iota(jnp.int32, (tm, ROUTER_LANES), 1)
        rank_grp = jnp.sum(jnp.where(lane_t == grp, rank_ref[...], 0.0), axis=1, keepdims=True)
        slot = (lax.broadcasted_iota(jnp.int32, (tm, MOE_CHUNK), 1) + k * MOE_CHUNK).astype(F32)
        sel = jnp.where(rank_grp == slot, 1.0, 0.0).astype(BF16)
        xc = lax.dot_general(sel, h_ref[...], contract0, preferred_element_type=F32).astype(BF16)
        cw = (lax.dot_general(sel, rh_ref[...], contract0, preferred_element_type=F32)
              + lax.dot_general(sel, rl_ref[...], contract0, preferred_element_type=F32))
        lane_c = lax.broadcasted_iota(jnp.int32, cw.shape, 1)
        yc = jnp.zeros((MOE_CHUNK, D_MODEL), F32)
        for e in range(MOE_EXPERTS_PER_GROUP):
            ge = grp * MOE_EXPERTS_PER_GROUP + e
            a = jnp.dot(xc, w1_ref[ge], preferred_element_type=F32)
            b = jnp.dot(xc, w3_ref[ge], preferred_element_type=F32)
            c = jnp.sum(jnp.where(lane_c == ROUTER_E0 + ge, cw, 0.0), axis=1, keepdims=True)
            hid = (a * _sigmoid(a) * b * c).astype(BF16)
            yc = yc + jnp.dot(hid, w2_ref[ge], preferred_element_type=F32)
        acc_ref[...] += jnp.dot(sel, yc.astype(BF16), preferred_element_type=F32)
        return grp

    def one_group(grp, carry):
        n_chunks = (cnt_ref[grp] + MOE_CHUNK - 1) // MOE_CHUNK
        lax.fori_loop(0, n_chunks, one_chunk, grp)
        return carry

    lax.fori_loop(0, MOE_GROUPS, one_group, 0)
    y = x + acc_ref[...]
    o_ref[...] = _rms(y, fg_ref[...]) if final else y


def _moe(x, g, wr_hi, wr_lo, br, w1, w3, w2, final_g, final):
    n_tok = x.shape[0]
    tm = MOE_TOKEN_TILE
    row = lambda i: (i, 0)
    once = lambda shape: pl.BlockSpec(shape, lambda i: (0,) * len(shape), pipeline_mode=pl.Buffered(1))
    earlier = np.tril(np.ones((tm, tm), np.float32), -1)
    return pl.pallas_call(
        functools.partial(_moe_body, final=final),
        grid=(n_tok // tm,),
        in_specs=[pl.BlockSpec((tm, D_MODEL), row), _const_spec((1, D_MODEL)),
                  _const_spec(wr_hi.shape), _const_spec(wr_lo.shape), _const_spec(br.shape), once((tm, tm)),
                  once(w1.shape), once(w3.shape), once(w2.shape), _const_spec((1, D_MODEL))],
        out_specs=pl.BlockSpec((tm, D_MODEL), row),
        out_shape=jax.ShapeDtypeStruct((n_tok, D_MODEL), F32),
        scratch_shapes=[pltpu.VMEM((tm, D_MODEL), BF16), pltpu.VMEM((tm, D_MODEL), F32),
                        pltpu.VMEM((tm, ROUTER_LANES), BF16), pltpu.VMEM((tm, ROUTER_LANES), BF16),
                        pltpu.VMEM((tm, ROUTER_LANES), F32), pltpu.SMEM((MOE_GROUPS,), jnp.int32)],
        compiler_params=_cparams(("parallel",), 60),
        name="moe",
    )(x, g, wr_hi, wr_lo, br, jnp.asarray(earlier, BF16), w1, w3, w2, final_g)


def _rope_tables(seq):
    t = jnp.arange(seq)
    pos = jnp.stack([t // GRID_W, t % GRID_W], axis=-1).astype(F32)
    inv = ROPE_THETA ** (-jnp.arange(0, ROPE_AXIS_DIM, 2, dtype=F32) / ROPE_AXIS_DIM)
    ang = pos[..., None] * inv
    ang = jnp.concatenate([ang, ang], axis=-1).reshape(seq, HEAD_DIM)
    first = (jnp.arange(HEAD_DIM) % ROPE_AXIS_DIM) < (ROPE_AXIS_DIM // 2)
    cos = jnp.cos(ang)
    sin = jnp.where(first[None, :], -jnp.sin(ang), jnp.sin(ang))
    reps = QK_COLS // HEAD_DIM
    return jnp.tile(cos, (1, reps)), jnp.tile(sin, (1, reps))


ATT_HEAD_ORDER = (0, 2, 1, 3)


def _pack_layer(l, w_in, na_rpb, ml_gate_b, ml_norm_g, conv_dw_w, conv_dw_b, conv_norm_g, conv_norm_b,
                att_q_norm_g, att_k_norm_g, w_branch, w_gate, b_gate, w_out, moe_w_group, moe_b_group,
                moe_w_expert, moe_b_expert, moe_w1, moe_w3, moe_w2):
    scale = HEAD_DIM ** -0.5
    w = w_in[l]
    o0 = 0
    wna = w[:, o0:o0 + NA_COLS]
    wna = jnp.concatenate([wna[:, :BRANCH_WIDTH] * scale, wna[:, BRANCH_WIDTH:]], axis=1)
    o0 += NA_COLS
    wml = w[:, o0:o0 + 4 * BRANCH_WIDTH]
    wml = jnp.concatenate([wml[:, :BRANCH_WIDTH], wml[:, BRANCH_WIDTH:2 * BRANCH_WIDTH] * scale,
                           wml[:, 2 * BRANCH_WIDTH:]], axis=1)
    o0 += 4 * BRANCH_WIDTH
    n_gates = ML_N_GATES * N_HEADS
    wg = jnp.pad(w[:, o0:o0 + n_gates], ((0, 0), (0, LANES - n_gates)))
    o0 += n_gates
    wcv = w[:, o0:o0 + CONV_COLS]
    o0 += CONV_COLS
    wat = w[:, o0:o0 + ATT_COLS]
    wq = wat[:, :N_HEADS * HEAD_DIM].reshape(D_MODEL, N_HEADS, HEAD_DIM)[:, ATT_HEAD_ORDER, :]
    wat = jnp.concatenate([wq.reshape(D_MODEL, N_HEADS * HEAD_DIM), wat[:, N_HEADS * HEAD_DIM:]], axis=1)
    qkg = jnp.concatenate([jnp.tile(att_q_norm_g[l] * scale, N_HEADS), jnp.tile(att_k_norm_g[l], ATT_KV_HEADS)])
    wb = w_branch[l]
    wb3 = wb[3].reshape(N_HEADS, HEAD_DIM, D_MODEL)[ATT_HEAD_ORDER, :, :].reshape(BRANCH_WIDTH, D_MODEL)
    wb = jnp.concatenate([wb[:3], wb3[None]], axis=0)
    wr = jnp.concatenate([moe_w_group[l], moe_w_expert[l].reshape(D_MODEL, MOE_N_EXPERTS)], axis=1)
    n_r = MOE_GROUPS + MOE_N_EXPERTS
    br = jnp.concatenate([moe_b_group[l], moe_b_expert[l].reshape(MOE_N_EXPERTS)])
    wr_hi, wr_lo = _split_bf16(jnp.pad(wr, ((0, 0), (0, ROUTER_LANES - n_r))))
    return dict(
        wna=wna.astype(BF16), wml=wml.astype(BF16), wg=wg.astype(BF16), wcv=wcv.astype(BF16), wat=wat.astype(BF16),
        qkg=qkg.reshape(1, QK_COLS).astype(F32),
        na_bias=_na_bias_table(na_rpb[l]),
        gate_b=jnp.pad(ml_gate_b[l].reshape(1, n_gates), ((0, 0), (0, LANES - n_gates))).astype(F32),
        ml_norm_g=ml_norm_g[l].reshape(1, BRANCH_WIDTH),
        conv_w=jnp.broadcast_to(conv_dw_w[l][:, None, :], (CONV_WIDTH, SUBLANES, BRANCH_WIDTH)), conv_b=conv_dw_b[l].reshape(1, BRANCH_WIDTH),
        conv_g=conv_norm_g[l].reshape(1, BRANCH_WIDTH), conv_beta=conv_norm_b[l].reshape(1, BRANCH_WIDTH),
        w_gate=w_gate[l].astype(BF16), b_gate=b_gate[l].reshape(1, N_BRANCHES * D_MODEL),
        w_branch=wb.astype(BF16), w_out=w_out[l].astype(BF16),
        wr_hi=wr_hi, wr_lo=wr_lo, br=jnp.pad(br, (0, ROUTER_LANES - n_r)).reshape(1, ROUTER_LANES),
        w1=moe_w1[l].astype(BF16), w3=moe_w3[l].astype(BF16), w2=moe_w2[l].astype(BF16),
    )


def _trunk(x, layers, norm_mix_g, norm_ffn_g, final_norm_g, ones_qk, depth):
    batch, seq, _ = x.shape
    xf = x.reshape(batch * seq, D_MODEL)
    cos_t, sin_t = _rope_tables(seq)
    fg = final_norm_g.reshape(1, D_MODEL)
    for l in range(depth):
        p = layers[l]
        gm = norm_mix_g[l].reshape(1, D_MODEL)
        u_na, u_ml, gates, u_cv, qk, v = _inproj(xf, gm, p["wna"], p["wml"], p["wg"], p["wcv"], p["wat"], ones_qk,
                                                 p["qkg"], cos_t, sin_t, seq)
        o_na = _na(u_na, p["na_bias"], batch, seq)
        o_ml = _mlstm(u_ml, gates, p["gate_b"], p["ml_norm_g"], batch, seq)
        o_cv = _conv(u_cv, p["conv_w"], p["conv_b"], p["conv_g"], p["conv_beta"], batch, seq)
        o_at = _gqa(qk, v, batch, seq)
        xf = _merge(xf, gm, (o_na, o_ml, o_cv, o_at), p["w_gate"], p["b_gate"], p["w_branch"], p["w_out"])
        xf = _moe(xf, norm_ffn_g[l].reshape(1, D_MODEL), p["wr_hi"], p["wr_lo"], p["br"], p["w1"], p["w3"], p["w2"], fg,
                  final=(l == depth - 1))
    return xf.reshape(batch, seq, D_MODEL)


def kernel(x_prompt, x_sample, norm_mix_g, w_in, na_rpb, ml_gate_b, ml_norm_g, conv_dw_w, conv_dw_b, conv_norm_g, conv_norm_b, att_q_norm_g, att_k_norm_g, w_branch, w_gate, b_gate, w_out, norm_ffn_g, moe_w_group, moe_b_group, moe_w_expert, moe_b_expert, moe_w1, moe_w3, moe_w2, final_norm_g):
    depth = w_in.shape[0]
    layers = [_pack_layer(l, w_in, na_rpb, ml_gate_b, ml_norm_g, conv_dw_w, conv_dw_b, conv_norm_g, conv_norm_b,
                          att_q_norm_g, att_k_norm_g, w_branch, w_gate, b_gate, w_out, moe_w_group, moe_b_group,
                          moe_w_expert, moe_b_expert, moe_w1, moe_w3, moe_w2) for l in range(depth)]
    hid = np.arange(QK_COLS) // HEAD_DIM
    ones_qk = jnp.asarray(hid[:, None] == hid[None, :], BF16)
    y_prompt = _trunk(x_prompt, layers, norm_mix_g, norm_ffn_g, final_norm_g, ones_qk, depth)
    y_sample = _trunk(x_sample, layers, norm_mix_g, norm_ffn_g, final_norm_g, ones_qk, depth)
    return (y_prompt, y_sample)
```
